```python
import math
import jax, jax.numpy as jnp
from jax import lax
import numpy as np

D_MODEL = 1024
BATCH = 16
SEQ = 256
DEPTH = 2
DEC_BATCH = 2
DEC_SEQ = 4096
PAST_LEN = 256

GRID_W = 64
W_CONV = 512
CONV_K = 31
W_HYENA = 512
HYENA_ORDER = 2
SHORT_K = 3
FILTER_BANDS = 16
FILTER_EMB = 1 + 2 * FILTER_BANDS
FILTER_HIDDEN = 64
DECAY_TARGET = 1e-2
FAST_DECAY_PCT = 0.3
SLOW_DECAY_PCT = 1.5
W_POOL = 512
POOL_WINDOWS = (2, 4, 8, 16)
POOL_GROUP = W_POOL // len(POOL_WINDOWS)
HEAD_DIM = 64
N_Q_HEADS = 16
N_KV_HEADS = 4
Q_PER_KV = N_Q_HEADS // N_KV_HEADS
W_ATTN = N_Q_HEADS * HEAD_DIM
W_KV = N_KV_HEADS * HEAD_DIM
ROPE_THETA = 10000.0
Q_BLOCK = 128

N_BRANCH = 4
EPS = 1e-6

SPLITS = (
    2 * W_CONV, W_CONV,
    (HYENA_ORDER + 1) * W_HYENA, W_HYENA,
    W_POOL, W_POOL,
    W_ATTN, W_KV, W_KV, W_ATTN,
    N_BRANCH * D_MODEL,
)
N_IN = sum(SPLITS)

kernel_name = "hybrid_flow_backbone_step"


def rms_norm(x, g):
    xf = x.astype(jnp.float32)
    y = xf * lax.rsqrt(jnp.mean(xf * xf, axis=-1, keepdims=True) + EPS)
    return (y * g.astype(jnp.float32)).astype(x.dtype)


def layer_norm(x, g, b):
    xf = x.astype(jnp.float32)
    mu = jnp.mean(xf, axis=-1, keepdims=True)
    xc = xf - mu
    y = xc * lax.rsqrt(jnp.mean(xc * xc, axis=-1, keepdims=True) + EPS)
    return (y * g.astype(jnp.float32) + b.astype(jnp.float32)).astype(x.dtype)


def depthwise_conv(x, w, b):
    k = w.shape[0]
    y = lax.conv_general_dilated(
        x, w[:, None, :], window_strides=(1,), padding=[(k // 2, k // 2)],
        dimension_numbers=('NWC', 'WIO', 'NWC'), feature_group_count=x.shape[-1])
    return y + b


def conformer_conv(u_glu, dw_w, dw_b, ln_g, ln_b, pw):
    val, gate = jnp.split(u_glu, 2, axis=-1)
    a = val * jax.nn.sigmoid(gate)
    a = depthwise_conv(a, dw_w, dw_b)
    a = jax.nn.silu(layer_norm(a, ln_g, ln_b))
    return a @ pw


def hyena_filters(L, w1, b1, freq, w2, b2, w3, b3):
    f32 = jnp.float32
    t = jnp.linspace(0.0, 1.0, L, dtype=f32)[:, None]
    bands = jnp.linspace(1e-4, FILTER_BANDS - 1, FILTER_BANDS, dtype=f32)[None, :]
    w = (2.0 * math.pi / L) * jnp.arange(L, dtype=f32)[:, None]
    z = jnp.concatenate([t, jnp.cos(bands * w), jnp.sin(bands * w)], axis=-1)
    fr = freq.astype(f32)
    hdn = jnp.sin(fr * (z @ w1.astype(f32) + b1.astype(f32)))
    hdn = jnp.sin(fr * (hdn @ w2.astype(f32) + b2.astype(f32)))
    h = hdn @ w3.astype(f32) + b3.astype(f32)
    h = h.reshape(L, 2, HYENA_ORDER, W_HYENA)
    max_decay = math.log(DECAY_TARGET) / FAST_DECAY_PCT
    min_decay = math.log(DECAY_TARGET) / SLOW_DECAY_PCT
    deltas = jnp.linspace(min_decay, max_decay, W_HYENA, dtype=f32)
    h = h * jnp.exp(-t * jnp.abs(deltas))[:, None, None, :]
    fwd, bwd = h[:, 0], h[:, 1]
    filt = jnp.concatenate(
        [fwd, jnp.zeros((1, HYENA_ORDER, W_HYENA), f32), bwd[1:][::-1]], axis=0)
    return filt * lax.rsqrt(jnp.sum(filt * filt, axis=0, keepdims=True) + EPS)


def hyena(u_proj, short_w, short_b, filt, skip):
    L = u_proj.shape[1]
    n = 2 * L
    uc = depthwise_conv(u_proj, short_w, short_b)
    v, x1, x2 = jnp.split(uc, 3, axis=-1)
    z = v
    for o, gate in enumerate((x1, x2)):
        zf32 = z.astype(jnp.float32)
        zf = jnp.fft.rfft(zf32, n=n, axis=1)
        hf = jnp.fft.rfft(filt[:, o], n=n, axis=0)
        y = jnp.fft.irfft(zf * hf[None], n=n, axis=1)[:, :L]
        z = gate * (y + zf32 * skip[o].astype(jnp.float32)).astype(gate.dtype)
    return z


def multiscale_pool(x, w, scale):
    B, L, _ = x.shape
    t = jnp.arange(L)
    xf = x.astype(jnp.float32)
    outs = []
    for g, win in enumerate(POOL_WINDOWS):
        xg = xf[..., g * POOL_GROUP:(g + 1) * POOL_GROUP]
        cs = jnp.concatenate(
            [jnp.zeros((B, 1, POOL_GROUP), jnp.float32), jnp.cumsum(xg, axis=1)], axis=1)
        lo = jnp.clip(t - win // 2, 0, L)
        hi = jnp.clip(t + win // 2, 0, L)
        cnt = (hi - lo).astype(jnp.float32)
        mean = (cs[:, hi] - cs[:, lo]) / cnt[None, :, None]
        outs.append(mean - xg)
    pooled = jnp.stack(outs, axis=2)
    y = jnp.einsum('blgc,gcd->blgd', pooled, w.astype(jnp.float32)).reshape(B, L, W_POOL)
    return (y * scale.astype(jnp.float32)).astype(x.dtype)


def rope_axis(x, pos):
    f = x.shape[-1] // 2
    inv = ROPE_THETA ** (-jnp.arange(f, dtype=jnp.float32) / f)
    ang = pos[:, None] * inv[None, :]
    cos = jnp.cos(ang)[None, :, None, :]
    sin = jnp.sin(ang)[None, :, None, :]
    x1, x2 = x[..., :f], x[..., f:]
    return jnp.concatenate([x1 * cos - x2 * sin, x2 * cos + x1 * sin], axis=-1)


def rope_2d(x):
    L = x.shape[1]
    rows = L // GRID_W
    row = jnp.repeat(jnp.arange(rows), GRID_W).astype(jnp.float32)
    col = jnp.tile(jnp.arange(GRID_W), rows).astype(jnp.float32)
    xf = x.astype(jnp.float32)
    half = HEAD_DIM // 2
    out = jnp.concatenate([rope_axis(xf[..., :half], row), rope_axis(xf[..., half:], col)], axis=-1)
    return out.astype(x.dtype)


def block_attention(q, k, v):
    B, Lq = q.shape[0], q.shape[1]
    nb = Lq // Q_BLOCK
    qb = q.reshape(B, nb, Q_BLOCK, N_KV_HEADS, Q_PER_KV, HEAD_DIM).transpose(1, 0, 2, 3, 4, 5)
    scale = HEAD_DIM ** -0.5

    def one_block(qi):
        s = jnp.einsum('bqkgd,bskd->bkgqs', qi, k, preferred_element_type=jnp.float32) * scale
        p = jax.nn.softmax(s, axis=-1)
        return jnp.einsum('bkgqs,bskd->bqkgd', p.astype(v.dtype), v)

    o = lax.map(one_block, qb)
    return o.transpose(1, 0, 2, 3, 4, 5).reshape(B, Lq, W_ATTN)


def parallel_mixer(h, p, latent, k_ctx, v_ctx):
    B, L, _ = h.shape
    u = h @ p['w_in']
    points = [int(s) for s in np.cumsum(SPLITS)[:-1]]
    (a_glu, a_gate, b_proj, b_gate, c_in, c_gate,
     q, k, v, d_gate, g_merge) = jnp.split(u, points, axis=-1)

    ya = conformer_conv(a_glu, p['conv_dw_w'], p['conv_dw_b'], p['conv_ln_g'],
                        p['conv_ln_b'], p['conv_pw'])
    ya = (ya * jax.nn.silu(a_gate)) @ p['wo_conv']

    filt = hyena_filters(L, p['hy_w1'], p['hy_b1'], p['hy_freq'], p['hy_w2'], p['hy_b2'],
                         p['hy_w3'], p['hy_b3'])
    yb = hyena(b_proj, p['hy_short_w'], p['hy_short_b'], filt, p['hy_skip'])
    yb = (yb * jax.nn.silu(b_gate)) @ p['wo_hyena']

    yc = multiscale_pool(c_in, p['pool_w'], p['pool_scale'])
    yc = (yc * jax.nn.silu(c_gate)) @ p['wo_pool']

    q = rms_norm(q.reshape(B, L, N_Q_HEADS, HEAD_DIM), p['q_norm'])
    k = rms_norm(k.reshape(B, L, N_KV_HEADS, HEAD_DIM), p['k_norm'])
    v = v.reshape(B, L, N_KV_HEADS, HEAD_DIM)
    if latent:
        q_r, k_r = rope_2d(q), rope_2d(k)
        k_all = jnp.concatenate([k_ctx, k_r], axis=1)
        v_all = jnp.concatenate([v_ctx, v], axis=1)
    else:
        q_r = q
        k_all, v_all = k, v
    att = block_attention(q_r.reshape(B, L, N_KV_HEADS, Q_PER_KV, HEAD_DIM), k_all, v_all)
    yd = (att * jax.nn.silu(d_gate)) @ p['wo_attn']

    gm = jax.nn.sigmoid(g_merge).reshape(B, L, N_BRANCH, D_MODEL)
    merged = gm[:, :, 0] * ya + gm[:, :, 1] * yb + gm[:, :, 2] * yc + gm[:, :, 3] * yd
    return merged @ p['w_out'], k, v


def trunk_layer(x, cond, p, latent, k_ctx, v_ctx):
    mod = jax.nn.silu(cond) @ p['w_ada'] + p['b_ada']
    shift, scale, gate = jnp.split(mod, 3, axis=-1)
    h = rms_norm(x, p['norm_g']) * (1.0 + scale) + shift
    y, k, v = parallel_mixer(h, p, latent, k_ctx, v_ctx)
    return x + gate * y, k, v


def setup_inputs(seed: int = 0) -> dict:
    key = jax.random.key(seed)
    ks = iter(jax.random.split(key, 64))
    f32 = jnp.float32

    def nrm(shape, scale):
        return jax.random.normal(next(ks), shape, f32) * scale

    def gain(shape, s=0.05):
        return 1.0 + nrm(shape, s)

    D = D_MODEL
    return {
        'x_prompt': nrm((BATCH, SEQ, D), 1.0),
        'x_sample': nrm((DEC_BATCH, DEC_SEQ, D), 1.0),
        'cache_k': nrm((DEC_BATCH, DEPTH, PAST_LEN, N_KV_HEADS, HEAD_DIM), 1.0),
        'cache_v': nrm((DEC_BATCH, DEPTH, PAST_LEN, N_KV_HEADS, HEAD_DIM), 1.0),
        'c': nrm((DEC_BATCH, D), 1.0),
        'c_ctx': nrm((D,), 1.0),
        'w_ada': nrm((DEPTH, D, 3 * D), 0.5 * D ** -0.5),
        'b_ada': nrm((DEPTH, 3 * D), 0.01),
        'norm_g': gain((DEPTH, D)),
        'w_in': nrm((DEPTH, D, N_IN), D ** -0.5),
        'conv_dw_w': nrm((DEPTH, CONV_K, W_CONV), CONV_K ** -0.5),
        'conv_dw_b': nrm((DEPTH, W_CONV), 0.01),
        'conv_ln_g': gain((DEPTH, W_CONV)),
        'conv_ln_b': nrm((DEPTH, W_CONV), 0.01),
        'conv_pw': nrm((DEPTH, W_CONV, W_CONV), W_CONV ** -0.5),
        'hy_short_w': nrm((DEPTH, SHORT_K, (HYENA_ORDER + 1) * W_HYENA), SHORT_K ** -0.5),
        'hy_short_b': nrm((DEPTH, (HYENA_ORDER + 1) * W_HYENA), 0.01),
        'hy_w1': nrm((DEPTH, FILTER_EMB, FILTER_HIDDEN), FILTER_EMB ** -0.5),
        'hy_b1': nrm((DEPTH, FILTER_HIDDEN), 0.1),
        'hy_freq': gain((DEPTH, FILTER_HIDDEN), 0.1),
        'hy_w2': nrm((DEPTH, FILTER_HIDDEN, FILTER_HIDDEN), FILTER_HIDDEN ** -0.5),
        'hy_b2': nrm((DEPTH, FILTER_HIDDEN), 0.1),
        'hy_w3': nrm((DEPTH, FILTER_HIDDEN, 2 * HYENA_ORDER * W_HYENA), FILTER_HIDDEN ** -0.5),
        'hy_b3': nrm((DEPTH, 2 * HYENA_ORDER * W_HYENA), 0.01),
        'hy_skip': nrm((DEPTH, HYENA_ORDER, W_HYENA), 0.5),
        'pool_w': nrm((DEPTH, len(POOL_WINDOWS), POOL_GROUP, POOL_GROUP), POOL_GROUP ** -0.5),
        'pool_scale': gain((DEPTH, W_POOL), 0.1),
        'q_norm': gain((DEPTH, HEAD_DIM)),
        'k_norm': gain((DEPTH, HEAD_DIM)),
        'wo_conv': nrm((DEPTH, W_CONV, D), W_CONV ** -0.5),
        'wo_hyena': nrm((DEPTH, W_HYENA, D), W_HYENA ** -0.5),
        'wo_pool': nrm((DEPTH, W_POOL, D), W_POOL ** -0.5),
        'wo_attn': nrm((DEPTH, W_ATTN, D), W_ATTN ** -0.5),
        'w_out': nrm((DEPTH, D, D), D ** -0.5),
    }


def reference(x_prompt, x_sample, cache_k, cache_v, c, c_ctx, w_ada, b_ada, norm_g, w_in,
              conv_dw_w, conv_dw_b, conv_ln_g, conv_ln_b, conv_pw,
              hy_short_w, hy_short_b, hy_w1, hy_b1, hy_freq, hy_w2, hy_b2, hy_w3, hy_b3, hy_skip,
              pool_w, pool_scale, q_norm, k_norm,
              wo_conv, wo_hyena, wo_pool, wo_attn, w_out):
    stacked = {
        'w_ada': w_ada, 'b_ada': b_ada, 'norm_g': norm_g, 'w_in': w_in,
        'conv_dw_w': conv_dw_w, 'conv_dw_b': conv_dw_b, 'conv_ln_g': conv_ln_g,
        'conv_ln_b': conv_ln_b, 'conv_pw': conv_pw,
        'hy_short_w': hy_short_w, 'hy_short_b': hy_short_b, 'hy_w1': hy_w1, 'hy_b1': hy_b1,
        'hy_freq': hy_freq, 'hy_w2': hy_w2, 'hy_b2': hy_b2, 'hy_w3': hy_w3, 'hy_b3': hy_b3,
        'hy_skip': hy_skip, 'pool_w': pool_w, 'pool_scale': pool_scale,
        'q_norm': q_norm, 'k_norm': k_norm,
        'wo_conv': wo_conv, 'wo_hyena': wo_hyena, 'wo_pool': wo_pool, 'wo_attn': wo_attn,
        'w_out': w_out,
    }

    y_prompt = x_prompt
    cond_ctx = c_ctx[None, None, :]
    ks, vs = [], []
    for l in range(DEPTH):
        p = {name: arr[l] for name, arr in stacked.items()}
        y_prompt, k_l, v_l = trunk_layer(y_prompt, cond_ctx, p, False, None, None)
        ks.append(k_l)
        vs.append(v_l)
    new_cache_k = jnp.stack(ks, axis=1)
    new_cache_v = jnp.stack(vs, axis=1)

    y_sample = x_sample
    cond = c[:, None, :]
    for l in range(DEPTH):
        p = {name: arr[l] for name, arr in stacked.items()}
        y_sample, _, _ = trunk_layer(y_sample, cond, p, True, cache_k[:, l], cache_v[:, l])

    return (y_prompt, y_sample, new_cache_k, new_cache_v)
```

```python
import functools
import math

import numpy as np
import jax
import jax.numpy as jnp
from jax import lax
from jax.experimental import pallas as pl
from jax.experimental.pallas import tpu as pltpu

f32 = jnp.float32
bf16 = jnp.bfloat16

D_MODEL = 1024
BATCH = 16
SEQ = 256
DEPTH = 2
DEC_BATCH = 2
DEC_SEQ = 4096
PAST_LEN = 256
GRID_W = 64
W_BR = 512
CONV_K = 31
SHORT_K = 3
FILTER_BANDS = 16
FILTER_EMB = 1 + 2 * FILTER_BANDS
FILTER_HIDDEN = 64
DECAY_TARGET = 1e-2
FAST_DECAY_PCT = 0.3
SLOW_DECAY_PCT = 1.5
POOL_WINDOWS = (2, 4, 8, 16)
POOL_GROUP = W_BR // len(POOL_WINDOWS)
HEAD_DIM = 64
N_Q_HEADS = 16
N_KV_HEADS = 4
Q_PER_KV = N_Q_HEADS // N_KV_HEADS
W_ATTN = N_Q_HEADS * HEAD_DIM
W_KV = N_KV_HEADS * HEAD_DIM
ROPE_THETA = 10000.0
EPS = 1e-6
N_IN = 2 * W_BR + W_BR + 3 * W_BR + W_BR + W_BR + W_BR + W_ATTN + 2 * W_KV + W_ATTN + 4 * D_MODEL

TM = 256
N_CTX_TOK = BATCH * SEQ
N_LAT_TOK = DEC_BATCH * DEC_SEQ
N_TOK = N_CTX_TOK + N_LAT_TOK
N_TILES = N_TOK // TM
CTX_TILES = N_CTX_TOK // TM
LAT_TPS = DEC_SEQ // TM
DFT_N1 = 512
DFT_HALF = DFT_N1 // 2
VMEM_LIMIT = 56 * 1024 * 1024

assert SEQ == TM and CTX_TILES % LAT_TPS == 0


def _sigmoid(x):
    return 1.0 / (1.0 + jnp.exp(-x))


def _silu(x):
    return x * _sigmoid(x)


def _mm(a, b):
    return jnp.dot(a, b, preferred_element_type=f32)


def _split_bf16(a):
    hi = a.astype(bf16)
    lo = (a - hi.astype(f32)).astype(bf16)
    return hi, lo


def _mm3(a, b):
    ah, al = _split_bf16(a)
    bh, bl = _split_bf16(b)
    return _mm(ah, bh) + (_mm(ah, bl) + _mm(al, bh))


def _params(n_axes, vmem=None):
    return pltpu.CompilerParams(dimension_semantics=("arbitrary",) * n_axes,
                                vmem_limit_bytes=vmem)


def _tile_position():
    i = pl.program_id(0)
    return i >= CTX_TILES, lax.rem(i, LAT_TPS)


def _halo_flags():
    is_lat, jt = _tile_position()
    return jnp.logical_and(is_lat, jt > 0), jnp.logical_and(is_lat, jt < LAT_TPS - 1)


def _halo_specs(width, hb):
    r = TM // hb
    last = N_TOK // hb - 1
    return [
        pl.BlockSpec((hb, width), lambda i: (jnp.maximum(i * r - 1, 0), 0)),
        pl.BlockSpec((TM, width), lambda i: (i, 0)),
        pl.BlockSpec((hb, width), lambda i: (jnp.minimum((i + 1) * r, last), 0)),
    ]


def _ada_kernel(c_ref, w_ref, b_ref, o_ref):
    c = c_ref[...]
    o_ref[...] = _mm(_silu(c).astype(bf16), w_ref[...].astype(bf16)) + b_ref[...]


def _ada_call(cond8, w_ada, b_ada):
    nb = 3 * D_MODEL // D_MODEL
    return pl.pallas_call(
        _ada_kernel,
        grid=(DEPTH, nb),
        in_specs=[
            pl.BlockSpec((8, D_MODEL), lambda l, j: (0, 0)),
            pl.BlockSpec((None, D_MODEL, D_MODEL), lambda l, j: (l, 0, j)),
            pl.BlockSpec((None, 1, D_MODEL), lambda l, j: (l, 0, j)),
        ],
        out_specs=pl.BlockSpec((None, 8, D_MODEL), lambda l, j: (l, 0, j)),
        out_shape=jax.ShapeDtypeStruct((DEPTH, 8, 3 * D_MODEL), f32),
        compiler_params=_params(2, VMEM_LIMIT),
        name="ada_mod",
    )(cond8, w_ada, b_ada.reshape(DEPTH, 1, 3 * D_MODEL))


_C_GLU = 0
_C_AGATE = 2 * W_BR
_C_BPROJ = _C_AGATE + W_BR
_C_BGATE = _C_BPROJ + 3 * W_BR
_C_CIN = _C_BGATE + W_BR
_C_CGATE = _C_CIN + W_BR
_C_Q = _C_CGATE + W_BR
_C_KV = _C_Q + W_ATTN
_C_DGATE = _C_KV + 2 * W_KV
_C_GM = _C_DGATE + W_ATTN
assert _C_GM + 4 * D_MODEL == N_IN


def _inproj_kernel(x_ref, mod_ref, g_ref, w_ref,
                   a_ref, sa_ref, bp_ref, sb_ref, ci_ref, sc_ref, q_ref, kv_ref, sd_ref, gm_ref):
    x = x_ref[...]
    ms = jnp.mean(x * x, axis=-1, keepdims=True)
    hn = x * lax.rsqrt(ms + EPS) * g_ref[...]
    h = (hn * (1.0 + mod_ref[1:2, :]) + mod_ref[0:1, :]).astype(bf16)

    def proj(c0, width):
        return _mm(h, w_ref[:, c0:c0 + width])

    a_ref[...] = proj(_C_GLU, W_BR) * _sigmoid(proj(_C_GLU + W_BR, W_BR))
    sa_ref[...] = _silu(proj(_C_AGATE, W_BR)).astype(bf16)
    for j in range(3):
        bp_ref[:, j * W_BR:(j + 1) * W_BR] = proj(_C_BPROJ + j * W_BR, W_BR)
    sb_ref[...] = _silu(proj(_C_BGATE, W_BR)).astype(bf16)
    ci_ref[...] = proj(_C_CIN, W_BR)
    sc_ref[...] = _silu(proj(_C_CGATE, W_BR)).astype(bf16)
    for j in range(W_ATTN // W_BR):
        q_ref[:, j * W_BR:(j + 1) * W_BR] = proj(_C_Q + j * W_BR, W_BR)
    kv_ref[...] = proj(_C_KV, 2 * W_KV)
    for j in range(W_ATTN // W_BR):
        sd_ref[:, j * W_BR:(j + 1) * W_BR] = _silu(proj(_C_DGATE + j * W_BR, W_BR)).astype(bf16)
    for j in range(4 * D_MODEL // W_BR):
        gm_ref[:, j * W_BR:(j + 1) * W_BR] = _sigmoid(proj(_C_GM + j * W_BR, W_BR)).astype(bf16)


def _inproj_call(x, mod_tiles_l, norm_g_l, w_in_l):
    widths = [(W_BR, f32), (W_BR, bf16), (3 * W_BR, f32), (W_BR, bf16), (W_BR, f32), (W_BR, bf16),
              (W_ATTN, f32), (2 * W_KV, f32), (W_ATTN, bf16), (4 * D_MODEL, bf16)]
    return pl.pallas_call(
        _inproj_kernel,
        grid=(N_TILES,),
        in_specs=[
            pl.BlockSpec((TM, D_MODEL), lambda i: (i, 0)),
            pl.BlockSpec((None, 3, D_MODEL), lambda i: (i, 0, 0)),
            pl.BlockSpec((1, D_MODEL), lambda i: (0, 0)),
            pl.BlockSpec((D_MODEL, N_IN), lambda i: (0, 0), pipeline_mode=pl.Buffered(1)),
        ],
        out_specs=[pl.BlockSpec((TM, w), lambda i: (i, 0)) for w, _ in widths],
        out_shape=[jax.ShapeDtypeStruct((N_TOK, w), dt) for w, dt in widths],
        compiler_params=_params(1, VMEM_LIMIT),
        name="in_proj",
    )(x, mod_tiles_l, norm_g_l.reshape(1, D_MODEL), w_in_l)


_CONF_HB = 16
_CONF_CHUNK = 32


def _conformer_kernel(prev_ref, cur_ref, next_ref, dw_ref, dwb_ref, lng_ref, lnb_ref, pw_ref,
                      o_ref, pad_ref, conv_ref):
    has_prev, has_next = _halo_flags()
    pad_ref[0:_CONF_HB, :] = jnp.where(has_prev, prev_ref[...], 0.0)
    pad_ref[_CONF_HB:_CONF_HB + TM, :] = cur_ref[...]
    pad_ref[_CONF_HB + TM:, :] = jnp.where(has_next, next_ref[...], 0.0)
    off = _CONF_HB - CONV_K // 2
    for r in range(TM // _CONF_CHUNK):
        r0 = r * _CONF_CHUNK
        acc = jnp.broadcast_to(dwb_ref[...], (_CONF_CHUNK, W_BR))
        for j in range(CONV_K):
            acc = acc + dw_ref[j:j + 1, :] * pad_ref[r0 + j + off:r0 + j + off + _CONF_CHUNK, :]
        conv_ref[r0:r0 + _CONF_CHUNK, :] = acc
    a = conv_ref[...]
    mu = jnp.mean(a, axis=-1, keepdims=True)
    xc = a - mu
    var = jnp.mean(xc * xc, axis=-1, keepdims=True)
    y = xc * lax.rsqrt(var + EPS) * lng_ref[...] + lnb_ref[...]
    o_ref[...] = _mm(_silu(y).astype(bf16), pw_ref[...]).astype(bf16)


def _conformer_call(a, dw_w, dw_b, ln_g, ln_b, pw_bf):
    row = lambda v: v.reshape(1, W_BR)
    const = lambda shape: pl.BlockSpec(shape, lambda i: (0, 0))
    return pl.pallas_call(
        _conformer_kernel,
        grid=(N_TILES,),
        in_specs=_halo_specs(W_BR, _CONF_HB) + [
            const((CONV_K, W_BR)), const((1, W_BR)), const((1, W_BR)), const((1, W_BR)),
            const((W_BR, W_BR))],
        out_specs=pl.BlockSpec((TM, W_BR), lambda i: (i, 0)),
        out_shape=jax.ShapeDtypeStruct((N_TOK, W_BR), bf16),
        scratch_shapes=[pltpu.VMEM((TM + 2 * _CONF_HB, W_BR), f32), pltpu.VMEM((TM, W_BR), f32)],
        compiler_params=_params(1),
        name="conformer",
    )(a, a, a, dw_w, row(dw_b), row(ln_g), row(ln_b), pw_bf)


_HB8 = 8


def _shortconv_kernel(prev_ref, cur_ref, next_ref, w_ref, b_ref, v_ref, x1_ref, x2_ref, pad_ref):
    has_prev, has_next = _halo_flags()
    pad_ref[0:_HB8, :] = jnp.where(has_prev, prev_ref[...], 0.0)
    pad_ref[_HB8:_HB8 + TM, :] = cur_ref[...]
    pad_ref[_HB8 + TM:, :] = jnp.where(has_next, next_ref[...], 0.0)
    outs = (v_ref, x1_ref, x2_ref)
    for g in range(3):
        cs = slice(g * W_BR, (g + 1) * W_BR)
        acc = jnp.broadcast_to(b_ref[:, cs], (TM, W_BR))
        for j in range(SHORT_K):
            r0 = _HB8 + j - SHORT_K // 2
            acc = acc + w_ref[j:j + 1, cs] * pad_ref[r0:r0 + TM, cs]
        outs[g][...] = acc


def _shortconv_call(bp, w, b):
    width = 3 * W_BR
    const = lambda shape: pl.BlockSpec(shape, lambda i: (0, 0))
    out = jax.ShapeDtypeStruct((N_TOK, W_BR), f32)
    return pl.pallas_call(
        _shortconv_kernel,
        grid=(N_TILES,),
        in_specs=_halo_specs(width, _HB8) + [const((SHORT_K, width)), const((1, width))],
        out_specs=[pl.BlockSpec((TM, W_BR), lambda i: (i, 0))] * 3,
        out_shape=[out, out, out],
        scratch_shapes=[pltpu.VMEM((TM + 2 * _HB8, width), f32)],
        compiler_params=_params(1),
        name="hyena_short",
    )(bp, bp, bp, w, b.reshape(1, width))


def _pool_kernel(prev_ref, cur_ref, next_ref, pw_ref, ps_ref, o_ref, pad_ref):
    has_prev, has_next = _halo_flags()
    is_lat, jt = _tile_position()
    pad_ref[0:_HB8, :] = jnp.where(has_prev, prev_ref[...], 0.0)
    pad_ref[_HB8:_HB8 + TM, :] = cur_ref[...]
    pad_ref[_HB8 + TM:, :] = jnp.where(has_next, next_ref[...], 0.0)
    seq_len = jnp.where(is_lat, DEC_SEQ, SEQ)
    t = jnp.where(is_lat, jt * TM, 0) + lax.broadcasted_iota(jnp.int32, (TM, POOL_GROUP), 0)
    for g, win in enumerate(POOL_WINDOWS):
        cs = slice(g * POOL_GROUP, (g + 1) * POOL_GROUP)
        hw = win // 2
        s = pad_ref[_HB8 - hw:_HB8 - hw + TM, cs]
        for d in range(-hw + 1, hw):
            s = s + pad_ref[_HB8 + d:_HB8 + d + TM, cs]
        cnt = (jnp.minimum(t + hw, seq_len) - jnp.maximum(t - hw, 0)).astype(f32)
        pooled = s / cnt - cur_ref[:, cs]
        y = _mm(pooled.astype(bf16), pw_ref[g])
        o_ref[:, cs] = (y * ps_ref[:, cs]).astype(bf16)


def _pool_call(ci, pool_w_bf, pool_scale):
    return pl.pallas_call(
        _pool_kernel,
        grid=(N_TILES,),
        in_specs=_halo_specs(W_BR, _HB8) + [
            pl.BlockSpec((len(POOL_WINDOWS), POOL_GROUP, POOL_GROUP), lambda i: (0, 0, 0)),
            pl.BlockSpec((1, W_BR), lambda i: (0, 0))],
        out_specs=pl.BlockSpec((TM, W_BR), lambda i: (i, 0)),
        out_shape=jax.ShapeDtypeStruct((N_TOK, W_BR), bf16),
        scratch_shapes=[pltpu.VMEM((TM + 2 * _HB8, W_BR), f32)],
        compiler_params=_params(1),
        name="pool",
    )(ci, ci, ci, pool_w_bf, pool_scale.reshape(1, W_BR))


def _dft_tables(seq_len):
    n = 2 * seq_len
    n2 = n // DFT_N1
    k1 = np.arange(DFT_N1, dtype=np.int64)[:, None]
    fwd, inv = [], []
    for t2 in range(n2):
        t = n2 * np.arange(DFT_HALF, dtype=np.int64)[None, :] + t2
        ang = -2.0 * np.pi * ((k1 * t) % n).astype(np.float64) / n
        gr, gi = np.cos(ang), np.sin(ang)
        fwd.append(np.concatenate([gr, gi], axis=0))
        inv.append(np.concatenate([gr.T, gi.T], axis=1))
    return (jnp.asarray(np.stack(fwd).astype(np.float32)).astype(bf16),
            jnp.asarray(np.stack(inv).astype(np.float32)).astype(bf16))


def _fft_list(xs, sign):
    n = len(xs)
    if n == 1:
        return list(xs)
    ev = _fft_list(xs[0::2], sign)
    od = _fft_list(xs[1::2], sign)
    out = [None] * n
    for k in range(n // 2):
        orr, oi = od[k]
        if k == 0:
            tr, ti = orr, oi
        elif 4 * k == n:
            tr, ti = (-oi, orr) if sign > 0 else (oi, -orr)
        else:
            ang = sign * 2.0 * math.pi * k / n
            c, s = math.cos(ang), math.sin(ang)
            tr = c * orr - s * oi
            ti = s * orr + c * oi
        er, ei = ev[k]
        out[k] = (er + tr, ei + ti)
        out[k + n // 2] = (er - tr, ei - ti)
    return out


_FEAT_PAD = 64


def _filter_features(seq_len):
    m = np.arange(2 * seq_len)
    j = np.where(m < seq_len, m, 2 * seq_len - m).astype(np.float64)
    t = j / (seq_len - 1)
    bands = np.linspace(1e-4, FILTER_BANDS - 1, FILTER_BANDS)[None, :]
    w = (2.0 * math.pi / seq_len) * j[:, None]
    z = np.concatenate([t[:, None], np.cos(bands * w), np.sin(bands * w)], axis=-1)
    z = np.pad(z, ((0, 0), (0, _FEAT_PAD - FILTER_EMB)))
    return jnp.asarray(z.astype(np.float32)), jnp.asarray(t.astype(np.float32)[:, None])


def _abs_deltas():
    max_decay = math.log(DECAY_TARGET) / FAST_DECAY_PCT
    min_decay = math.log(DECAY_TARGET) / SLOW_DECAY_PCT
    d = np.abs(np.linspace(min_decay, max_decay, W_BR))
    return jnp.asarray(np.concatenate([d, d]).astype(np.float32)[None, :])


def _filter_mlp_kernel(seq_len, tr, z_ref, t_ref, w1_ref, b1_ref, fr_ref, w2_ref, b2_ref, w3_ref, b3_ref,
                       ad_ref, h_ref, ss_ref):
    i = pl.program_id(0)
    fr = fr_ref[...]
    hdn = jnp.sin(fr * (_mm3(z_ref[...], w1_ref[...]) + b1_ref[...]))
    hdn = jnp.sin(fr * (_mm3(hdn, w2_ref[...]) + b2_ref[...]))
    h = _mm3(hdn, w3_ref[...]) + b3_ref[...]
    h = h * jnp.exp(-t_ref[...] * ad_ref[...])
    m = i * tr + lax.broadcasted_iota(jnp.int32, h.shape, 0)
    h = jnp.where(m == seq_len, 0.0, h)
    h_ref[...] = h

    @pl.when(i == 0)
    def _():
        ss_ref[...] = jnp.zeros_like(ss_ref)

    ss_ref[...] += jnp.sum(h * h, axis=0, keepdims=True)


def _filter_mlp_call(seq_len, feats, tcol, w1p, b1, freq, w2, b2, w3, b3, absd):
    tr = min(512, seq_len)
    steps = 2 * seq_len // tr
    half_steps = seq_len // tr
    wide = 2 * W_BR
    const = lambda shape: pl.BlockSpec(shape, lambda i: (0, 0))
    row = lambda v: v.reshape(1, -1)
    return pl.pallas_call(
        functools.partial(_filter_mlp_kernel, seq_len, tr),
        grid=(steps,),
        in_specs=[
            pl.BlockSpec((tr, _FEAT_PAD), lambda i: (i, 0)),
            pl.BlockSpec((tr, 1), lambda i: (i, 0)),
            const((_FEAT_PAD, FILTER_HIDDEN)), const((1, FILTER_HIDDEN)), const((1, FILTER_HIDDEN)),
            const((FILTER_HIDDEN, FILTER_HIDDEN)), const((1, FILTER_HIDDEN)),
            pl.BlockSpec((FILTER_HIDDEN, wide), lambda i: (0, i // half_steps)),
            pl.BlockSpec((1, wide), lambda i: (0, i // half_steps)),
            const((1, wide)),
        ],
        out_specs=[pl.BlockSpec((tr, wide), lambda i: (i, 0)), const((1, wide))],
        out_shape=[jax.ShapeDtypeStruct((2 * seq_len, wide), f32), jax.ShapeDtypeStruct((1, wide), f32)],
        compiler_params=_params(1),
        name="filter_mlp",
    )(feats, tcol, w1p, row(b1), row(freq), w2, row(b2), w3, row(b3), absd)


def _filter_dft1_kernel(g_ref, h1_ref, h2_ref, ar_ref, ai_ref):
    z = jnp.concatenate([h1_ref[...], h2_ref[...]], axis=1).astype(bf16)
    p = _mm(g_ref[...], z)
    wide = 2 * W_BR
    odd = jnp.bitwise_and(lax.broadcasted_iota(jnp.int32, (DFT_N1, wide), 0), 1) == 1
    ar_ref[...] = p[:DFT_N1, :wide] + jnp.where(odd, -p[:DFT_N1, wide:], p[:DFT_N1, wide:])
    ai_ref[...] = p[DFT_N1:, :wide] + jnp.where(odd, -p[DFT_N1:, wide:], p[DFT_N1:, wide:])


def _filter_dft2_kernel(n2, scale_const, ar_ref, ai_ref, ss_ref, hr_ref, hi_ref):
    xs = _fft_list([(ar_ref[t], ai_ref[t]) for t in range(n2)], -1)
    scale = lax.rsqrt(ss_ref[...] + EPS) * scale_const
    for k in range(n2):
        hr_ref[k] = xs[k][0] * scale
        hi_ref[k] = xs[k][1] * scale


def _filter_spectrum(seq_len, gfwd, hfilt, ss):
    n2 = 2 * seq_len // DFT_N1
    wide = 2 * W_BR
    hv = hfilt.reshape(2, DFT_HALF, n2 * wide)
    spec_shape = jax.ShapeDtypeStruct((n2, DFT_N1, wide), f32)
    ar, ai = pl.pallas_call(
        _filter_dft1_kernel,
        grid=(n2,),
        in_specs=[
            pl.BlockSpec((None, 2 * DFT_N1, DFT_HALF), lambda t: (t, 0, 0)),
            pl.BlockSpec((None, DFT_HALF, wide), lambda t: (0, 0, t)),
            pl.BlockSpec((None, DFT_HALF, wide), lambda t: (1, 0, t)),
        ],
        out_specs=[pl.BlockSpec((None, DFT_N1, wide), lambda t: (t, 0, 0))] * 2,
        out_shape=[spec_shape, spec_shape],
        compiler_params=_params(1),
        name="filter_dft1",
    )(gfwd, hv, hv)
    rb, lb = 64, 128
    blk = pl.BlockSpec((n2, rb, lb), lambda r, c: (0, r, c))
    return pl.pallas_call(
        functools.partial(_filter_dft2_kernel, n2, 1.0 / (2 * seq_len)),
        grid=(DFT_N1 // rb, wide // lb),
        in_specs=[blk, blk, pl.BlockSpec((1, lb), lambda r, c: (0, c))],
        out_specs=[blk, blk],
        out_shape=[spec_shape, spec_shape],
        compiler_params=_params(2),
        name="filter_dft2",
    )(ar, ai, ss)


def _pair_specs(src):
    _, ia, ib = src
    lead = (None,) * (len(ia(0, 0)) - 2)
    return [pl.BlockSpec(lead + (DFT_HALF, W_BR), ia), pl.BlockSpec(lead + (DFT_HALF, W_BR), ib)]


def _lc_fwd_kernel(g_ref, za_ref, zb_ref, ar_ref, ai_ref):
    z = jnp.concatenate([za_ref[...], zb_ref[...]], axis=1).astype(bf16)
    p = _mm(g_ref[...], z)
    ar_ref[...] = p[:DFT_N1, :W_BR] - p[DFT_N1:, W_BR:]
    ai_ref[...] = p[:DFT_N1, W_BR:] + p[DFT_N1:, :W_BR]


def _lc_fwd_call(n_pairs, n2, gfwd, z_src):
    spec_shape = jax.ShapeDtypeStruct((n_pairs, n2, DFT_N1, W_BR), f32)
    out_spec = pl.BlockSpec((None, None, DFT_N1, W_BR), lambda p, t: (p, t, 0, 0))
    return pl.pallas_call(
        _lc_fwd_kernel,
        grid=(n_pairs, n2),
        in_specs=[pl.BlockSpec((None, 2 * DFT_N1, DFT_HALF), lambda p, t: (t, 0, 0))] + _pair_specs(z_src),
        out_specs=[out_spec, out_spec],
        out_shape=[spec_shape, spec_shape],
        compiler_params=_params(2),
        name="lc_fwd",
    )(gfwd, z_src[0], z_src[0])


_MID_ROWS = 8


def _lc_mid_kernel(n2, rb, ar_ref, ai_ref, hr_ref, hi_ref, br_ref, bi_ref):
    def body(r, carry):
        rs = pl.ds(pl.multiple_of(r * _MID_ROWS, _MID_ROWS), _MID_ROWS)
        xs = _fft_list([(ar_ref[t, rs, :], ai_ref[t, rs, :]) for t in range(n2)], -1)
        ys = []
        for k in range(n2):
            xr, xi = xs[k]
            fr, fi = hr_ref[k, rs, :], hi_ref[k, rs, :]
            ys.append((xr * fr - xi * fi, xr * fi + xi * fr))
        bs = _fft_list(ys, 1)
        for t in range(n2):
            br_ref[t, rs, :] = bs[t][0]
            bi_ref[t, rs, :] = bs[t][1]
        return carry

    lax.fori_loop(0, rb // _MID_ROWS, body, 0)


def _lc_mid_call(n_pairs, n2, ar, ai, hr, hi, order):
    rb, lb = 128, 128
    lane_blocks = W_BR // lb
    dspec = pl.BlockSpec((None, n2, rb, lb), lambda p, r, c: (p, 0, r, c))
    hspec = pl.BlockSpec((n2, rb, lb), lambda p, r, c: (0, r, order * lane_blocks + c))
    shape = jax.ShapeDtypeStruct((n_pairs, n2, DFT_N1, W_BR), f32)
    return pl.pallas_call(
        functools.partial(_lc_mid_kernel, n2, rb),
        grid=(n_pairs, DFT_N1 // rb, lane_blocks),
        in_specs=[dspec, dspec, hspec, hspec],
        out_specs=[dspec, dspec],
        out_shape=[shape, shape],
        compiler_params=_params(3),
        name="lc_mid",
    )(ar, ai, hr, hi)


def _lc_inv_kernel(final, gt_ref, br_ref, bi_ref, za_ref, zb_ref, ga_ref, gb_ref, skip_ref, *rest):
    if final:
        sa_ref, sb_ref, o_ref = rest
    else:
        (o_ref,) = rest
    br = br_ref[...]
    bi = bi_ref[...]
    rhs = jnp.concatenate([jnp.concatenate([br, bi], axis=1),
                           jnp.concatenate([bi, -br], axis=1)], axis=0).astype(bf16)
    y = _mm(gt_ref[...], rhs)
    skip = skip_ref[...]
    za = ga_ref[...] * (y[:, :W_BR] + za_ref[...] * skip)
    zb = gb_ref[...] * (y[:, W_BR:] + zb_ref[...] * skip)
    if final:
        o_ref[0] = (za * sa_ref[...].astype(f32)).astype(o_ref.dtype)
        o_ref[1] = (zb * sb_ref[...].astype(f32)).astype(o_ref.dtype)
    else:
        o_ref[0] = za
        o_ref[1] = zb


def _lc_inv_call(n_pairs, n2, ginv, br, bi, z_src, gate_src, skip, silu_src=None):
    final = silu_src is not None
    bspec = pl.BlockSpec((None, None, DFT_N1, W_BR), lambda p, t: (p, t, 0, 0))
    in_specs = ([pl.BlockSpec((None, DFT_HALF, 2 * DFT_N1), lambda p, t: (t, 0, 0)), bspec, bspec]
                + _pair_specs(z_src) + _pair_specs(gate_src)
                + [pl.BlockSpec((1, W_BR), lambda p, t: (0, 0))])
    args = [ginv, br, bi, z_src[0], z_src[0], gate_src[0], gate_src[0], skip.reshape(1, W_BR)]
    if final:
        in_specs += _pair_specs(silu_src)
        args += [silu_src[0], silu_src[0]]
    return pl.pallas_call(
        functools.partial(_lc_inv_kernel, final),
        grid=(n_pairs, n2),
        in_specs=in_specs,
        out_specs=pl.BlockSpec((None, 2, DFT_HALF, W_BR), lambda p, t: (p, 0, 0, t)),
        out_shape=jax.ShapeDtypeStruct((n_pairs, 2, DFT_HALF, n2 * W_BR), bf16 if final else f32),
        compiler_params=_params(2),
        name="lc_inv",
    )(*args)


def _hyena_group(n_pairs, n2, seq0, tabs, spec, v, x1, x2, sb, skip):
    gfwd, ginv = tabs
    hr, hi = spec
    seq_len = DFT_HALF * n2
    rows = N_TOK // seq_len

    def tok_src(arr):
        a3 = arr.reshape(rows, DFT_HALF, n2 * W_BR)
        return (a3, lambda p, t: (seq0 + 2 * p, 0, t), lambda p, t: (seq0 + 2 * p + 1, 0, t))

    def pair_src(arr):
        return (arr, lambda p, t: (p, 0, 0, t), lambda p, t: (p, 1, 0, t))

    z_src = tok_src(v)
    for order, gate in enumerate((x1, x2)):
        ar, ai = _lc_fwd_call(n_pairs, n2, gfwd, z_src)
        br, bi = _lc_mid_call(n_pairs, n2, ar, ai, hr, hi, order)
        last = order == 1
        z = _lc_inv_call(n_pairs, n2, ginv, br, bi, z_src, tok_src(gate), skip[order],
                         tok_src(sb) if last else None)
        z_src = pair_src(z)
    return z.reshape(2 * n_pairs * seq_len, W_BR)


_QK_SCALE = HEAD_DIM ** -0.5 * math.log2(math.e)


def _rope_tables():
    half = HEAD_DIM // 2
    f = half // 2
    inv = ROPE_THETA ** (-np.arange(f, dtype=np.float64) / f)
    pos = np.arange(DEC_SEQ)
    row, col = (pos // GRID_W).astype(np.float64), (pos % GRID_W).astype(np.float64)
    ang = np.concatenate([np.tile(row[:, None] * inv[None, :], (1, 2)),
                          np.tile(col[:, None] * inv[None, :], (1, 2))], axis=1)
    sign = np.tile(np.concatenate([-np.ones(f), np.ones(f)]), 2)[None, :]
    cos = np.concatenate([np.ones((TM, HEAD_DIM)), np.cos(ang)], axis=0)
    sin = np.concatenate([np.zeros((TM, HEAD_DIM)), np.sin(ang) * sign], axis=0)
    rep = W_KV // HEAD_DIM
    return (jnp.asarray(np.tile(cos, (1, rep)).astype(np.float32)),
            jnp.asarray(np.tile(sin, (1, rep)).astype(np.float32)))


def _group_sum_matrix():
    idx = np.arange(W_KV) // HEAD_DIM
    return jnp.asarray((idx[:, None] == idx[None, :]).astype(np.float32)).astype(bf16)


def _qk_prep_kernel(q_ref, kv_ref, gq_ref, gk_ref, cos_ref, sin_ref, bd_ref,
                    qr_ref, kn_ref, kr_ref, vb_ref):
    cos = cos_ref[...]
    sin = sin_ref[...]
    bd = bd_ref[...]
    lane = lax.broadcasted_iota(jnp.int32, (TM, W_KV), 1)
    first_half = jnp.bitwise_and(lane, HEAD_DIM // 2 - 1) < HEAD_DIM // 4
    quarter = HEAD_DIM // 4

    def norm(x, g):
        hi, lo = _split_bf16(x * x)
        ss = _mm(hi, bd) + _mm(lo, bd)
        return x * lax.rsqrt(ss * (1.0 / HEAD_DIM) + EPS) * g

    def rope(x):
        swapped = jnp.where(first_half, pltpu.roll(x, W_KV - quarter, 1), pltpu.roll(x, quarter, 1))
        return x * cos + swapped * sin

    for j in range(W_ATTN // W_KV):
        cs = slice(j * W_KV, (j + 1) * W_KV)
        qn = norm(q_ref[:, cs], gq_ref[...])
        qr_ref[:, cs] = (rope(qn) * _QK_SCALE).astype(bf16)
    kn = norm(kv_ref[:, :W_KV], gk_ref[...])
    kn_ref[...] = kn
    kr_ref[...] = rope(kn).astype(bf16)
    vb_ref[...] = kv_ref[:, W_KV:].astype(bf16)


def _qk_prep_call(q, kv, q_norm, k_norm, cos_t, sin_t, bd):
    rep = W_KV // HEAD_DIM
    const = lambda shape: pl.BlockSpec(shape, lambda i: (0, 0))
    tab = pl.BlockSpec((TM, W_KV), lambda i: (jnp.where(i < CTX_TILES, 0, 1 + lax.rem(i, LAT_TPS)), 0))
    tile = lambda w: pl.BlockSpec((TM, w), lambda i: (i, 0))
    return pl.pallas_call(
        _qk_prep_kernel,
        grid=(N_TILES,),
        in_specs=[tile(W_ATTN), tile(2 * W_KV), const((1, W_KV)), const((1, W_KV)), tab, tab,
                  const((W_KV, W_KV))],
        out_specs=[tile(W_ATTN), tile(W_KV), tile(W_KV), tile(W_KV)],
        out_shape=[jax.ShapeDtypeStruct((N_TOK, W_ATTN), bf16), jax.ShapeDtypeStruct((N_TOK, W_KV), f32),
                   jax.ShapeDtypeStruct((N_TOK, W_KV), bf16), jax.ShapeDtypeStruct((N_TOK, W_KV), bf16)],
        compiler_params=_params(1),
        name="qk_prep",
    )(q, kv, jnp.tile(q_norm, rep).reshape(1, W_KV), jnp.tile(k_norm, rep).reshape(1, W_KV),
      cos_t, sin_t, bd)


def _attn_kernel(hps, qt_ref, k_ref, vt_ref, o_ref):
    for h in range(hps):
        g = h // Q_PER_KV
        s = _mm(k_ref[g], qt_ref[h])
        m = jnp.max(s, axis=0, keepdims=True)
        p = jnp.exp2(s - m)
        l = jnp.sum(p, axis=0, keepdims=True)
        o = _mm(vt_ref[g], p.astype(bf16))
        o_ref[h] = (o / l).astype(bf16)


def _attn_call(qt, k, vt, n_seq, seq_len, hps, tq, col0):
    n_keys = k.shape[2]
    kvps = max(1, hps // Q_PER_KV)
    hb = N_Q_HEADS // hps
    qtiles = seq_len // tq
    qoff = col0 // tq
    return pl.pallas_call(
        functools.partial(_attn_kernel, hps),
        grid=(n_seq, hb, qtiles),
        in_specs=[
            pl.BlockSpec((hps, HEAD_DIM, tq), lambda b, h, t: (h, 0, qoff + b * qtiles + t)),
            pl.BlockSpec((None, kvps, n_keys, HEAD_DIM), lambda b, h, t: (b, (h * hps) // (Q_PER_KV * kvps), 0, 0)),
            pl.BlockSpec((None, kvps, HEAD_DIM, n_keys), lambda b, h, t: (b, (h * hps) // (Q_PER_KV * kvps), 0, 0)),
        ],
        out_specs=pl.BlockSpec((hps, HEAD_DIM, tq), lambda b, h, t: (h, 0, b * qtiles + t)),
        out_shape=jax.ShapeDtypeStruct((N_Q_HEADS, HEAD_DIM, n_seq * seq_len), bf16),
        compiler_params=_params(3, VMEM_LIMIT),
        name="attention",
    )(qt, k, vt)


def _merge_kernel(x_ref, mod_ref, pa_ref, pb_ref, pc_ref, att_ref, sa_ref, sc_ref, sd_ref, gm_ref,
                  woa_ref, wob_ref, woc_ref, wod_ref, wout_ref, o_ref):
    def gated(p_ref, s_ref):
        return (p_ref[...].astype(f32) * s_ref[...].astype(f32)).astype(bf16)

    def gm(j):
        return gm_ref[:, j * D_MODEL:(j + 1) * D_MODEL].astype(f32)

    merged = gm(0) * _mm(gated(pa_ref, sa_ref), woa_ref[...])
    merged = merged + gm(1) * _mm(pb_ref[...], wob_ref[...])
    merged = merged + gm(2) * _mm(gated(pc_ref, sc_ref), woc_ref[...])
    merged = merged + gm(3) * _mm(gated(att_ref, sd_ref), wod_ref[...])
    y = _mm(merged.astype(bf16), wout_ref[...])
    o_ref[...] = x_ref[...] + mod_ref[2:3, :] * y


def _merge_call(x, mod_tiles_l, pa, pb, pc, att, sa, sc, sd, gm, woa, wob, woc, wod, wout):
    tile = lambda w: pl.BlockSpec((TM, w), lambda i: (i, 0))
    const = lambda shape: pl.BlockSpec(shape, lambda i: (0, 0))
    return pl.pallas_call(
        _merge_kernel,
        grid=(N_TILES,),
        in_specs=[tile(D_MODEL), pl.BlockSpec((None, 3, D_MODEL), lambda i: (i, 0, 0)),
                  tile(W_BR), tile(W_BR), tile(W_BR), tile(W_ATTN),
                  tile(W_BR), tile(W_BR), tile(W_ATTN), tile(4 * D_MODEL),
                  const((W_BR, D_MODEL)), const((W_BR, D_MODEL)), const((W_BR, D_MODEL)),
                  const((W_ATTN, D_MODEL)), const((D_MODEL, D_MODEL))],
        out_specs=tile(D_MODEL),
        out_shape=jax.ShapeDtypeStruct((N_TOK, D_MODEL), f32),
        compiler_params=_params(1, VMEM_LIMIT),
        name="merge_out",
    )(x, mod_tiles_l, pa, pb, pc, att, sa, sc, sd, gm, woa, wob, woc, wod, wout)


def _heads_major(x, n_seq, seq_len):
    return x.reshape(n_seq, seq_len, N_KV_HEADS, HEAD_DIM).transpose(0, 2, 1, 3)


def kernel(x_prompt, x_sample, cache_k, cache_v, c, c_ctx, w_ada, b_ada, norm_g, w_in, conv_dw_w, conv_dw_b, conv_ln_g, conv_ln_b, conv_pw, hy_short_w, hy_short_b, hy_w1, hy_b1, hy_freq, hy_w2, hy_b2, hy_w3, hy_b3, hy_skip, pool_w, pool_scale, q_norm, k_norm, wo_conv, wo_hyena, wo_pool, wo_attn, w_out):
    x = jnp.concatenate([x_prompt.reshape(N_CTX_TOK, D_MODEL), x_sample.reshape(N_LAT_TOK, D_MODEL)], axis=0)

    cond8 = jnp.concatenate([c_ctx[None, :], c, jnp.zeros((8 - 1 - DEC_BATCH, D_MODEL), f32)], axis=0)
    mod = _ada_call(cond8, w_ada, b_ada)
    tile_cond = np.concatenate([np.zeros(CTX_TILES, np.int32),
                                1 + np.arange(N_TILES - CTX_TILES, dtype=np.int32) // LAT_TPS])
    mod_tiles = mod[:, tile_cond].reshape(DEPTH, N_TILES, 3, D_MODEL)

    groups = ((BATCH // 2, SEQ, 0), (DEC_BATCH // 2, DEC_SEQ, N_CTX_TOK // DEC_SEQ))
    tabs = {L: _dft_tables(L) for _, L, _ in groups}
    feats = {L: _filter_features(L) for _, L, _ in groups}
    absd = _abs_deltas()
    cos_t, sin_t = _rope_tables()
    bd = _group_sum_matrix()
    w1p = jnp.pad(hy_w1, ((0, 0), (0, _FEAT_PAD - FILTER_EMB), (0, 0)))

    ks, vs = [], []
    for l in range(DEPTH):
        (a, sa, bp, sb, ci, sc, q, kv, sd, gm) = _inproj_call(x, mod_tiles[l], norm_g[l], w_in[l].astype(bf16))

        pa = _conformer_call(a, conv_dw_w[l], conv_dw_b[l], conv_ln_g[l], conv_ln_b[l], conv_pw[l].astype(bf16))
        pc = _pool_call(ci, pool_w[l].astype(bf16), pool_scale[l])

        v, x1, x2 = _shortconv_call(bp, hy_short_w[l], hy_short_b[l])
        pb_parts = []
        for n_pairs, L, seq0 in groups:
            hfilt, ss = _filter_mlp_call(L, feats[L][0], feats[L][1], w1p[l], hy_b1[l], hy_freq[l], hy_w2[l],
                                         hy_b2[l], hy_w3[l], hy_b3[l], absd)
            spec = _filter_spectrum(L, tabs[L][0], hfilt, ss)
            pb_parts.append(_hyena_group(n_pairs, 2 * L // DFT_N1, seq0, tabs[L], spec, v, x1, x2, sb,
                                         hy_skip[l]))
        pb = jnp.concatenate(pb_parts, axis=0)

        qr, kn, kr, vb = _qk_prep_call(q, kv, q_norm[l], k_norm[l], cos_t, sin_t, bd)
        qt = qr.reshape(N_TOK, N_Q_HEADS, HEAD_DIM).transpose(1, 2, 0)
        k_ctx = _heads_major(kr[:N_CTX_TOK], BATCH, SEQ)
        vt_ctx = _heads_major(vb[:N_CTX_TOK], BATCH, SEQ).transpose(0, 1, 3, 2)
        att_ctx = _attn_call(qt, k_ctx, vt_ctx, BATCH, SEQ, N_Q_HEADS, SEQ, 0)
        k_lat = jnp.concatenate([cache_k[:, l].astype(bf16).transpose(0, 2, 1, 3),
                                 _heads_major(kr[N_CTX_TOK:], DEC_BATCH, DEC_SEQ)], axis=2)
        v_lat = jnp.concatenate([cache_v[:, l].astype(bf16).transpose(0, 2, 1, 3),
                                 _heads_major(vb[N_CTX_TOK:], DEC_BATCH, DEC_SEQ)], axis=2)
        att_lat = _attn_call(qt, k_lat, v_lat.transpose(0, 1, 3, 2), DEC_BATCH, DEC_SEQ, 1, 512, N_CTX_TOK)
        att = jnp.concatenate([att_ctx, att_lat], axis=2).transpose(2, 0, 1).reshape(N_TOK, W_ATTN)

        x = _merge_call(x, mod_tiles[l], pa, pb, pc, att, sa, sc, sd, gm,
                        wo_conv[l].astype(bf16), wo_hyena[l].astype(bf16), wo_pool[l].astype(bf16),
                        wo_attn[l].astype(bf16), w_out[l].astype(bf16))

        ks.append(kn[:N_CTX_TOK].reshape(BATCH, SEQ, N_KV_HEADS, HEAD_DIM))
        vs.append(kv[:N_CTX_TOK, W_KV:].reshape(BATCH, SEQ, N_KV_HEADS, HEAD_DIM))

    y_prompt = x[:N_CTX_TOK].reshape(BATCH, SEQ, D_MODEL)
    y_sample = x[N_CTX_TOK:].reshape(DEC_BATCH, DEC_SEQ, D_MODEL)
    return (y_prompt, y_sample, jnp.stack(ks, axis=1), jnp.stack(vs, axis=1))
```

```python
import functools
import math

import numpy as np
import jax
import jax.numpy as jnp
from jax import lax
from jax.experimental import pallas as pl
from jax.experimental.pallas import tpu as pltpu

f32 = jnp.float32
bf16 = jnp.bfloat16

D_MODEL = 1024
BATCH = 16
SEQ = 256
DEPTH = 2
DEC_BATCH = 2
DEC_SEQ = 4096
PAST_LEN = 256
GRID_W = 64
W_BR = 512
CONV_K = 31
SHORT_K = 3
FILTER_BANDS = 16
FILTER_EMB = 1 + 2 * FILTER_BANDS
FILTER_HIDDEN = 64
DECAY_TARGET = 1e-2
FAST_DECAY_PCT = 0.3
SLOW_DECAY_PCT = 1.5
POOL_WINDOWS = (2, 4, 8, 16)
POOL_GROUP = W_BR // len(POOL_WINDOWS)
HEAD_DIM = 64
N_Q_HEADS = 16
N_KV_HEADS = 4
Q_PER_KV = N_Q_HEADS // N_KV_HEADS
W_ATTN = N_Q_HEADS * HEAD_DIM
W_KV = N_KV_HEADS * HEAD_DIM
ROPE_THETA = 10000.0
EPS = 1e-6
N_IN = 2 * W_BR + W_BR + 3 * W_BR + W_BR + W_BR + W_BR + W_ATTN + 2 * W_KV + W_ATTN + 4 * D_MODEL

TM = 256
N_CTX_TOK = BATCH * SEQ
N_LAT_TOK = DEC_BATCH * DEC_SEQ
N_TOK = N_CTX_TOK + N_LAT_TOK
N_TILES = N_TOK // TM
CTX_TILES = N_CTX_TOK // TM
LAT_TPS = DEC_SEQ // TM
DFT_N1 = 512
DFT_HALF = DFT_N1 // 2
VMEM_LIMIT = 56 * 1024 * 1024

assert SEQ == TM and CTX_TILES % LAT_TPS == 0


def _sigmoid(x):
    return 1.0 / (1.0 + jnp.exp(-x))


def _silu(x):
    return x * _sigmoid(x)


def _mm(a, b):
    return jnp.dot(a, b, preferred_element_type=f32)


def _split_bf16(a):
    hi = a.astype(bf16)
    lo = (a - hi.astype(f32)).astype(bf16)
    return hi, lo


def _mm3(a, b):
    ah, al = _split_bf16(a)
    bh, bl = _split_bf16(b)
    return _mm(ah, bh) + (_mm(ah, bl) + _mm(al, bh))


def _params(n_axes, vmem=None):
    return pltpu.CompilerParams(dimension_semantics=("arbitrary",) * n_axes,
                                vmem_limit_bytes=vmem)


def _tile_position():
    i = pl.program_id(0)
    return i >= CTX_TILES, lax.rem(i, LAT_TPS)


def _halo_flags():
    is_lat, jt = _tile_position()
    return jnp.logical_and(is_lat, jt > 0), jnp.logical_and(is_lat, jt < LAT_TPS - 1)


def _halo_specs(width, hb):
    r = TM // hb
    last = N_TOK // hb - 1
    return [
        pl.BlockSpec((hb, width), lambda i: (jnp.maximum(i * r - 1, 0), 0)),
        pl.BlockSpec((TM, width), lambda i: (i, 0)),
        pl.BlockSpec((hb, width), lambda i: (jnp.minimum((i + 1) * r, last), 0)),
    ]


def _ada_kernel(c_ref, w_ref, b_ref, o_ref):
    c = c_ref[...]
    o_ref[...] = _mm(_silu(c).astype(bf16), w_ref[...].astype(bf16)) + b_ref[...]


def _ada_call(cond8, w_ada, b_ada):
    nb = 3 * D_MODEL // D_MODEL
    return pl.pallas_call(
        _ada_kernel,
        grid=(DEPTH, nb),
        in_specs=[
            pl.BlockSpec((8, D_MODEL), lambda l, j: (0, 0)),
            pl.BlockSpec((None, D_MODEL, D_MODEL), lambda l, j: (l, 0, j)),
            pl.BlockSpec((None, 1, D_MODEL), lambda l, j: (l, 0, j)),
        ],
        out_specs=pl.BlockSpec((None, 8, D_MODEL), lambda l, j: (l, 0, j)),
        out_shape=jax.ShapeDtypeStruct((DEPTH, 8, 3 * D_MODEL), f32),
        compiler_params=_params(2, VMEM_LIMIT),
        name="ada_mod",
    )(cond8, w_ada, b_ada.reshape(DEPTH, 1, 3 * D_MODEL))


_C_GLU = 0
_C_AGATE = 2 * W_BR
_C_BPROJ = _C_AGATE + W_BR
_C_BGATE = _C_BPROJ + 3 * W_BR
_C_CIN = _C_BGATE + W_BR
_C_CGATE = _C_CIN + W_BR
_C_Q = _C_CGATE + W_BR
_C_KV = _C_Q + W_ATTN
_C_DGATE = _C_KV + 2 * W_KV
_C_GM = _C_DGATE + W_ATTN
assert _C_GM + 4 * D_MODEL == N_IN


def _inproj_kernel(x_ref, mod_ref, g_ref, w_ref,
                   a_ref, sa_ref, bp_ref, sb_ref, ci_ref, sc_ref, q_ref, kv_ref, sd_ref, gm_ref):
    x = x_ref[...]
    ms = jnp.mean(x * x, axis=-1, keepdims=True)
    hn = x * lax.rsqrt(ms + EPS) * g_ref[...]
    h = (hn * (1.0 + mod_ref[1:2, :]) + mod_ref[0:1, :]).astype(bf16)

    def proj(c0, width):
        return _mm(h, w_ref[:, c0:c0 + width])

    a_ref[...] = proj(_C_GLU, W_BR) * _sigmoid(proj(_C_GLU + W_BR, W_BR))
    sa_ref[...] = _silu(proj(_C_AGATE, W_BR)).astype(bf16)
    for j in range(3):
        bp_ref[:, j * W_BR:(j + 1) * W_BR] = proj(_C_BPROJ + j * W_BR, W_BR)
    sb_ref[...] = _silu(proj(_C_BGATE, W_BR)).astype(bf16)
    ci_ref[...] = proj(_C_CIN, W_BR)
    sc_ref[...] = _silu(proj(_C_CGATE, W_BR)).astype(bf16)
    for j in range(W_ATTN // W_BR):
        q_ref[:, j * W_BR:(j + 1) * W_BR] = proj(_C_Q + j * W_BR, W_BR)
    kv_ref[...] = proj(_C_KV, 2 * W_KV)
    for j in range(W_ATTN // W_BR):
        sd_ref[:, j * W_BR:(j + 1) * W_BR] = _silu(proj(_C_DGATE + j * W_BR, W_BR)).astype(bf16)
    for j in range(4 * D_MODEL // W_BR):
        gm_ref[:, j * W_BR:(j + 1) * W_BR] = _sigmoid(proj(_C_GM + j * W_BR, W_BR)).astype(bf16)


def _inproj_call(x, mod_tiles_l, norm_g_l, w_in_l):
    widths = [(W_BR, f32), (W_BR, bf16), (3 * W_BR, f32), (W_BR, bf16), (W_BR, f32), (W_BR, bf16),
              (W_ATTN, f32), (2 * W_KV, f32), (W_ATTN, bf16), (4 * D_MODEL, bf16)]
    return pl.pallas_call(
        _inproj_kernel,
        grid=(N_TILES,),
        in_specs=[
            pl.BlockSpec((TM, D_MODEL), lambda i: (i, 0)),
            pl.BlockSpec((None, 3, D_MODEL), lambda i: (i, 0, 0)),
            pl.BlockSpec((1, D_MODEL), lambda i: (0, 0)),
            pl.BlockSpec((D_MODEL, N_IN), lambda i: (0, 0), pipeline_mode=pl.Buffered(1)),
        ],
        out_specs=[pl.BlockSpec((TM, w), lambda i: (i, 0)) for w, _ in widths],
        out_shape=[jax.ShapeDtypeStruct((N_TOK, w), dt) for w, dt in widths],
        compiler_params=_params(1, VMEM_LIMIT),
        name="in_proj",
    )(x, mod_tiles_l, norm_g_l.reshape(1, D_MODEL), w_in_l)


_CONF_HB = 16
_CONF_CHUNK = 32


def _conformer_kernel(prev_ref, cur_ref, next_ref, dw_ref, dwb_ref, lng_ref, lnb_ref, pw_ref,
                      o_ref, pad_ref, conv_ref):
    has_prev, has_next = _halo_flags()
    pad_ref[0:_CONF_HB, :] = jnp.where(has_prev, prev_ref[...], 0.0)
    pad_ref[_CONF_HB:_CONF_HB + TM, :] = cur_ref[...]
    pad_ref[_CONF_HB + TM:, :] = jnp.where(has_next, next_ref[...], 0.0)
    off = _CONF_HB - CONV_K // 2
    for r in range(TM // _CONF_CHUNK):
        r0 = r * _CONF_CHUNK
        acc = jnp.broadcast_to(dwb_ref[...], (_CONF_CHUNK, W_BR))
        for j in range(CONV_K):
            acc = acc + dw_ref[j:j + 1, :] * pad_ref[r0 + j + off:r0 + j + off + _CONF_CHUNK, :]
        conv_ref[r0:r0 + _CONF_CHUNK, :] = acc
    a = conv_ref[...]
    mu = jnp.mean(a, axis=-1, keepdims=True)
    xc = a - mu
    var = jnp.mean(xc * xc, axis=-1, keepdims=True)
    y = xc * lax.rsqrt(var + EPS) * lng_ref[...] + lnb_ref[...]
    o_ref[...] = _mm(_silu(y).astype(bf16), pw_ref[...]).astype(bf16)


def _conformer_call(a, dw_w, dw_b, ln_g, ln_b, pw_bf):
    row = lambda v: v.reshape(1, W_BR)
    const = lambda shape: pl.BlockSpec(shape, lambda i: (0, 0))
    return pl.pallas_call(
        _conformer_kernel,
        grid=(N_TILES,),
        in_specs=_halo_specs(W_BR, _CONF_HB) + [
            const((CONV_K, W_BR)), const((1, W_BR)), const((1, W_BR)), const((1, W_BR)),
            const((W_BR, W_BR))],
        out_specs=pl.BlockSpec((TM, W_BR), lambda i: (i, 0)),
        out_shape=jax.ShapeDtypeStruct((N_TOK, W_BR), bf16),
        scratch_shapes=[pltpu.VMEM((TM + 2 * _CONF_HB, W_BR), f32), pltpu.VMEM((TM, W_BR), f32)],
        compiler_params=_params(1),
        name="conformer",
    )(a, a, a, dw_w, row(dw_b), row(ln_g), row(ln_b), pw_bf)


_HB8 = 8


def _shortconv_kernel(prev_ref, cur_ref, next_ref, w_ref, b_ref, v_ref, x1_ref, x2_ref, pad_ref):
    has_prev, has_next = _halo_flags()
    pad_ref[0:_HB8, :] = jnp.where(has_prev, prev_ref[...], 0.0)
    pad_ref[_HB8:_HB8 + TM, :] = cur_ref[...]
    pad_ref[_HB8 + TM:, :] = jnp.where(has_next, next_ref[...], 0.0)
    outs = (v_ref, x1_ref, x2_ref)
    for g in range(3):
        cs = slice(g * W_BR, (g + 1) * W_BR)
        acc = jnp.broadcast_to(b_ref[:, cs], (TM, W_BR))
        for j in range(SHORT_K):
            r0 = _HB8 + j - SHORT_K // 2
            acc = acc + w_ref[j:j + 1, cs] * pad_ref[r0:r0 + TM, cs]
        outs[g][...] = acc


def _shortconv_call(bp, w, b):
    width = 3 * W_BR
    const = lambda shape: pl.BlockSpec(shape, lambda i: (0, 0))
    out = jax.ShapeDtypeStruct((N_TOK, W_BR), f32)
    return pl.pallas_call(
        _shortconv_kernel,
        grid=(N_TILES,),
        in_specs=_halo_specs(width, _HB8) + [const((SHORT_K, width)), const((1, width))],
        out_specs=[pl.BlockSpec((TM, W_BR), lambda i: (i, 0))] * 3,
        out_shape=[out, out, out],
        scratch_shapes=[pltpu.VMEM((TM + 2 * _HB8, width), f32)],
        compiler_params=_params(1),
        name="hyena_short",
    )(bp, bp, bp, w, b.reshape(1, width))


def _pool_kernel(prev_ref, cur_ref, next_ref, pw_ref, ps_ref, o_ref, pad_ref):
    has_prev, has_next = _halo_flags()
    is_lat, jt = _tile_position()
    pad_ref[0:_HB8, :] = jnp.where(has_prev, prev_ref[...], 0.0)
    pad_ref[_HB8:_HB8 + TM, :] = cur_ref[...]
    pad_ref[_HB8 + TM:, :] = jnp.where(has_next, next_ref[...], 0.0)
    seq_len = jnp.where(is_lat, DEC_SEQ, SEQ)
    t = jnp.where(is_lat, jt * TM, 0) + lax.broadcasted_iota(jnp.int32, (TM, POOL_GROUP), 0)
    for g, win in enumerate(POOL_WINDOWS):
        cs = slice(g * POOL_GROUP, (g + 1) * POOL_GROUP)
        hw = win // 2
        s = pad_ref[_HB8 - hw:_HB8 - hw + TM, cs]
        for d in range(-hw + 1, hw):
            s = s + pad_ref[_HB8 + d:_HB8 + d + TM, cs]
        cnt = (jnp.minimum(t + hw, seq_len) - jnp.maximum(t - hw, 0)).astype(f32)
        pooled = s / cnt - cur_ref[:, cs]
        y = _mm(pooled.astype(bf16), pw_ref[g])
        o_ref[:, cs] = (y * ps_ref[:, cs]).astype(bf16)


def _pool_call(ci, pool_w_bf, pool_scale):
    return pl.pallas_call(
        _pool_kernel,
        grid=(N_TILES,),
        in_specs=_halo_specs(W_BR, _HB8) + [
            pl.BlockSpec((len(POOL_WINDOWS), POOL_GROUP, POOL_GROUP), lambda i: (0, 0, 0)),
            pl.BlockSpec((1, W_BR), lambda i: (0, 0))],
        out_specs=pl.BlockSpec((TM, W_BR), lambda i: (i, 0)),
        out_shape=jax.ShapeDtypeStruct((N_TOK, W_BR), bf16),
        scratch_shapes=[pltpu.VMEM((TM + 2 * _HB8, W_BR), f32)],
        compiler_params=_params(1),
        name="pool",
    )(ci, ci, ci, pool_w_bf, pool_scale.reshape(1, W_BR))


def _dft_tables(seq_len):
    n = 2 * seq_len
    n2 = n // DFT_N1
    k1 = np.arange(DFT_N1, dtype=np.int64)[:, None]
    fwd, inv = [], []
    for t2 in range(n2):
        t = n2 * np.arange(DFT_HALF, dtype=np.int64)[None, :] + t2
        ang = -2.0 * np.pi * ((k1 * t) % n).astype(np.float64) / n
        gr, gi = np.cos(ang), np.sin(ang)
        fwd.append(np.concatenate([gr, gi], axis=0))
        inv.append(np.concatenate([gr.T, gi.T], axis=1))
    return (jnp.asarray(np.stack(fwd).astype(np.float32)).astype(bf16),
            jnp.asarray(np.stack(inv).astype(np.float32)).astype(bf16))


def _fft_list(xs, sign):
    n = len(xs)
    if n == 1:
        return list(xs)
    ev = _fft_list(xs[0::2], sign)
    od = _fft_list(xs[1::2], sign)
    out = [None] * n
    for k in range(n // 2):
        orr, oi = od[k]
        if k == 0:
            tr, ti = orr, oi
        elif 4 * k == n:
            tr, ti = (-oi, orr) if sign > 0 else (oi, -orr)
        else:
            ang = sign * 2.0 * math.pi * k / n
            c, s = math.cos(ang), math.sin(ang)
            tr = c * orr - s * oi
            ti = s * orr + c * oi
        er, ei = ev[k]
        out[k] = (er + tr, ei + ti)
        out[k + n // 2] = (er - tr, ei - ti)
    return out


_FEAT_PAD = 64


def _filter_features(seq_len):
    m = np.arange(2 * seq_len)
    j = np.where(m < seq_len, m, 2 * seq_len - m).astype(np.float64)
    t = j / (seq_len - 1)
    bands = np.linspace(1e-4, FILTER_BANDS - 1, FILTER_BANDS)[None, :]
    w = (2.0 * math.pi / seq_len) * j[:, None]
    z = np.concatenate([t[:, None], np.cos(bands * w), np.sin(bands * w)], axis=-1)
    z = np.pad(z, ((0, 0), (0, _FEAT_PAD - FILTER_EMB)))
    return jnp.asarray(z.astype(np.float32)), jnp.asarray(t.astype(np.float32)[:, None])


def _abs_deltas():
    max_decay = math.log(DECAY_TARGET) / FAST_DECAY_PCT
    min_decay = math.log(DECAY_TARGET) / SLOW_DECAY_PCT
    d = np.abs(np.linspace(min_decay, max_decay, W_BR))
    return jnp.asarray(np.concatenate([d, d]).astype(np.float32)[None, :])


def _filter_mlp_kernel(seq_len, tr, z_ref, t_ref, w1_ref, b1_ref, fr_ref, w2_ref, b2_ref, w3_ref, b3_ref,
                       ad_ref, h_ref, ss_ref):
    i = pl.program_id(0)
    fr = fr_ref[...]
    hdn = jnp.sin(fr * (_mm3(z_ref[...], w1_ref[...]) + b1_ref[...]))
    hdn = jnp.sin(fr * (_mm3(hdn, w2_ref[...]) + b2_ref[...]))
    h = _mm3(hdn, w3_ref[...]) + b3_ref[...]
    h = h * jnp.exp(-t_ref[...] * ad_ref[...])
    m = i * tr + lax.broadcasted_iota(jnp.int32, h.shape, 0)
    h = jnp.where(m == seq_len, 0.0, h)
    h_ref[...] = h

    @pl.when(i == 0)
    def _():
        ss_ref[...] = jnp.zeros_like(ss_ref)

    ss_ref[...] += jnp.sum(h * h, axis=0, keepdims=True)


def _filter_mlp_call(seq_len, feats, tcol, w1p, b1, freq, w2, b2, w3, b3, absd):
    tr = min(512, seq_len)
    steps = 2 * seq_len // tr
    half_steps = seq_len // tr
    wide = 2 * W_BR
    const = lambda shape: pl.BlockSpec(shape, lambda i: (0, 0))
    row = lambda v: v.reshape(1, -1)
    return pl.pallas_call(
        functools.partial(_filter_mlp_kernel, seq_len, tr),
        grid=(steps,),
        in_specs=[
            pl.BlockSpec((tr, _FEAT_PAD), lambda i: (i, 0)),
            pl.BlockSpec((tr, 1), lambda i: (i, 0)),
            const((_FEAT_PAD, FILTER_HIDDEN)), const((1, FILTER_HIDDEN)), const((1, FILTER_HIDDEN)),
            const((FILTER_HIDDEN, FILTER_HIDDEN)), const((1, FILTER_HIDDEN)),
            pl.BlockSpec((FILTER_HIDDEN, wide), lambda i: (0, i // half_steps)),
            pl.BlockSpec((1, wide), lambda i: (0, i // half_steps)),
            const((1, wide)),
        ],
        out_specs=[pl.BlockSpec((tr, wide), lambda i: (i, 0)), const((1, wide))],
        out_shape=[jax.ShapeDtypeStruct((2 * seq_len, wide), f32), jax.ShapeDtypeStruct((1, wide), f32)],
        compiler_params=_params(1),
        name="filter_mlp",
    )(feats, tcol, w1p, row(b1), row(freq), w2, row(b2), w3, row(b3), absd)


def _filter_dft1_kernel(g_ref, h1_ref, h2_ref, ar_ref, ai_ref):
    z = jnp.concatenate([h1_ref[...], h2_ref[...]], axis=1).astype(bf16)
    p = _mm(g_ref[...], z)
    wide = 2 * W_BR
    odd = jnp.bitwise_and(lax.broadcasted_iota(jnp.int32, (DFT_N1, wide), 0), 1) == 1
    ar_ref[...] = p[:DFT_N1, :wide] + jnp.where(odd, -p[:DFT_N1, wide:], p[:DFT_N1, wide:])
    ai_ref[...] = p[DFT_N1:, :wide] + jnp.where(odd, -p[DFT_N1:, wide:], p[DFT_N1:, wide:])


def _filter_dft2_kernel(n2, scale_const, ar_ref, ai_ref, ss_ref, hr_ref, hi_ref):
    xs = _fft_list([(ar_ref[t], ai_ref[t]) for t in range(n2)], -1)
    scale = lax.rsqrt(ss_ref[...] + EPS) * scale_const
    for k in range(n2):
        hr_ref[k] = xs[k][0] * scale
        hi_ref[k] = xs[k][1] * scale


def _filter_spectrum(seq_len, gfwd, hfilt, ss):
    n2 = 2 * seq_len // DFT_N1
    wide = 2 * W_BR
    hv = hfilt.reshape(2, DFT_HALF, n2 * wide)
    spec_shape = jax.ShapeDtypeStruct((n2, DFT_N1, wide), f32)
    ar, ai = pl.pallas_call(
        _filter_dft1_kernel,
        grid=(n2,),
        in_specs=[
            pl.BlockSpec((None, 2 * DFT_N1, DFT_HALF), lambda t: (t, 0, 0)),
            pl.BlockSpec((None, DFT_HALF, wide), lambda t: (0, 0, t)),
            pl.BlockSpec((None, DFT_HALF, wide), lambda t: (1, 0, t)),
        ],
        out_specs=[pl.BlockSpec((None, DFT_N1, wide), lambda t: (t, 0, 0))] * 2,
        out_shape=[spec_shape, spec_shape],
        compiler_params=_params(1),
        name="filter_dft1",
    )(gfwd, hv, hv)
    rb, lb = 64, 128
    blk = pl.BlockSpec((n2, rb, lb), lambda r, c: (0, r, c))
    return pl.pallas_call(
        functools.partial(_filter_dft2_kernel, n2, 1.0 / (2 * seq_len)),
        grid=(DFT_N1 // rb, wide // lb),
        in_specs=[blk, blk, pl.BlockSpec((1, lb), lambda r, c: (0, c))],
        out_specs=[blk, blk],
        out_shape=[spec_shape, spec_shape],
        compiler_params=_params(2),
        name="filter_dft2",
    )(ar, ai, ss)


def _pair_specs(src):
    _, ia, ib = src
    lead = (None,) * (len(ia(0, 0)) - 2)
    return [pl.BlockSpec(lead + (DFT_HALF, W_BR), ia), pl.BlockSpec(lead + (DFT_HALF, W_BR), ib)]


def _lc_fwd_kernel(g_ref, za_ref, zb_ref, ar_ref, ai_ref):
    z = jnp.concatenate([za_ref[...], zb_ref[...]], axis=1).astype(bf16)
    p = _mm(g_ref[...], z)
    ar_ref[...] = p[:DFT_N1, :W_BR] - p[DFT_N1:, W_BR:]
    ai_ref[...] = p[:DFT_N1, W_BR:] + p[DFT_N1:, :W_BR]


def _lc_fwd_call(n_pairs, n2, gfwd, z_src):
    spec_shape = jax.ShapeDtypeStruct((n_pairs, n2, DFT_N1, W_BR), f32)
    out_spec = pl.BlockSpec((None, None, DFT_N1, W_BR), lambda p, t: (p, t, 0, 0))
    return pl.pallas_call(
        _lc_fwd_kernel,
        grid=(n_pairs, n2),
        in_specs=[pl.BlockSpec((None, 2 * DFT_N1, DFT_HALF), lambda p, t: (t, 0, 0))] + _pair_specs(z_src),
        out_specs=[out_spec, out_spec],
        out_shape=[spec_shape, spec_shape],
        compiler_params=_params(2),
        name="lc_fwd",
    )(gfwd, z_src[0], z_src[0])


_MID_ROWS = 8


def _lc_mid_kernel(n2, rb, ar_ref, ai_ref, hr_ref, hi_ref, br_ref, bi_ref):
    def body(r, carry):
        rs = pl.ds(pl.multiple_of(r * _MID_ROWS, _MID_ROWS), _MID_ROWS)
        xs = _fft_list([(ar_ref[t, rs, :], ai_ref[t, rs, :]) for t in range(n2)], -1)
        ys = []
        for k in range(n2):
            xr, xi = xs[k]
            fr, fi = hr_ref[k, rs, :], hi_ref[k, rs, :]
            ys.append((xr * fr - xi * fi, xr * fi + xi * fr))
        bs = _fft_list(ys, 1)
        for t in range(n2):
            br_ref[t, rs, :] = bs[t][0]
            bi_ref[t, rs, :] = bs[t][1]
        return carry

    lax.fori_loop(0, rb // _MID_ROWS, body, 0)


def _lc_mid_call(n_pairs, n2, ar, ai, hr, hi, order):
    rb, lb = 128, 128
    lane_blocks = W_BR // lb
    dspec = pl.BlockSpec((None, n2, rb, lb), lambda p, r, c: (p, 0, r, c))
    hspec = pl.BlockSpec((n2, rb, lb), lambda p, r, c: (0, r, order * lane_blocks + c))
    shape = jax.ShapeDtypeStruct((n_pairs, n2, DFT_N1, W_BR), f32)
    return pl.pallas_call(
        functools.partial(_lc_mid_kernel, n2, rb),
        grid=(n_pairs, DFT_N1 // rb, lane_blocks),
        in_specs=[dspec, dspec, hspec, hspec],
        out_specs=[dspec, dspec],
        out_shape=[shape, shape],
        compiler_params=_params(3),
        name="lc_mid",
    )(ar, ai, hr, hi)


def _lc_inv_kernel(final, gt_ref, br_ref, bi_ref, za_ref, zb_ref, ga_ref, gb_ref, skip_ref, *rest):
    if final:
        sa_ref, sb_ref, o_ref = rest
    else:
        (o_ref,) = rest
    br = br_ref[...]
    bi = bi_ref[...]
    rhs = jnp.concatenate([jnp.concatenate([br, bi], axis=1),
                           jnp.concatenate([bi, -br], axis=1)], axis=0).astype(bf16)
    y = _mm(gt_ref[...], rhs)
    skip = skip_ref[...]
    za = ga_ref[...] * (y[:, :W_BR] + za_ref[...] * skip)
    zb = gb_ref[...] * (y[:, W_BR:] + zb_ref[...] * skip)
    if final:
        o_ref[0] = (za * sa_ref[...].astype(f32)).astype(o_ref.dtype)
        o_ref[1] = (zb * sb_ref[...].astype(f32)).astype(o_ref.dtype)
    else:
        o_ref[0] = za
        o_ref[1] = zb


def _lc_inv_call(n_pairs, n2, ginv, br, bi, z_src, gate_src, skip, silu_src=None):
    final = silu_src is not None
    bspec = pl.BlockSpec((None, None, DFT_N1, W_BR), lambda p, t: (p, t, 0, 0))
    in_specs = ([pl.BlockSpec((None, DFT_HALF, 2 * DFT_N1), lambda p, t: (t, 0, 0)), bspec, bspec]
                + _pair_specs(z_src) + _pair_specs(gate_src)
                + [pl.BlockSpec((1, W_BR), lambda p, t: (0, 0))])
    args = [ginv, br, bi, z_src[0], z_src[0], gate_src[0], gate_src[0], skip.reshape(1, W_BR)]
    if final:
        in_specs += _pair_specs(silu_src)
        args += [silu_src[0], silu_src[0]]
    return pl.pallas_call(
        functools.partial(_lc_inv_kernel, final),
        grid=(n_pairs, n2),
        in_specs=in_specs,
        out_specs=pl.BlockSpec((None, 2, DFT_HALF, W_BR), lambda p, t: (p, 0, 0, t)),
        out_shape=jax.ShapeDtypeStruct((n_pairs, 2, DFT_HALF, n2 * W_BR), bf16 if final else f32),
        compiler_params=_params(2),
        name="lc_inv",
    )(*args)


def _hyena_group(n_pairs, n2, seq0, tabs, spec, v, x1, x2, sb, skip):
    gfwd, ginv = tabs
    hr, hi = spec
    seq_len = DFT_HALF * n2
    rows = N_TOK // seq_len

    def tok_src(arr):
        a3 = arr.reshape(rows, DFT_HALF, n2 * W_BR)
        return (a3, lambda p, t: (seq0 + 2 * p, 0, t), lambda p, t: (seq0 + 2 * p + 1, 0, t))

    def pair_src(arr):
        return (arr, lambda p, t: (p, 0, 0, t), lambda p, t: (p, 1, 0, t))

    z_src = tok_src(v)
    for order, gate in enumerate((x1, x2)):
        ar, ai = _lc_fwd_call(n_pairs, n2, gfwd, z_src)
        br, bi = _lc_mid_call(n_pairs, n2, ar, ai, hr, hi, order)
        last = order == 1
        z = _lc_inv_call(n_pairs, n2, ginv, br, bi, z_src, tok_src(gate), skip[order],
                         tok_src(sb) if last else None)
        z_src = pair_src(z)
    return z.reshape(2 * n_pairs * seq_len, W_BR)


_QK_SCALE = HEAD_DIM ** -0.5 * math.log2(math.e)


def _rope_tables():
    half = HEAD_DIM // 2
    f = half // 2
    inv = ROPE_THETA ** (-np.arange(f, dtype=np.float64) / f)
    pos = np.arange(DEC_SEQ)
    row, col = (pos // GRID_W).astype(np.float64), (pos % GRID_W).astype(np.float64)
    ang = np.concatenate([np.tile(row[:, None] * inv[None, :], (1, 2)),
                          np.tile(col[:, None] * inv[None, :], (1, 2))], axis=1)
    sign = np.tile(np.concatenate([-np.ones(f), np.ones(f)]), 2)[None, :]
    cos = np.concatenate([np.ones((TM, HEAD_DIM)), np.cos(ang)], axis=0)
    sin = np.concatenate([np.zeros((TM, HEAD_DIM)), np.sin(ang) * sign], axis=0)
    rep = W_KV // HEAD_DIM
    return (jnp.asarray(np.tile(cos, (1, rep)).astype(np.float32)),
            jnp.asarray(np.tile(sin, (1, rep)).astype(np.float32)))


def _group_sum_matrix():
    idx = np.arange(W_KV) // HEAD_DIM
    return jnp.asarray((idx[:, None] == idx[None, :]).astype(np.float32)).astype(bf16)


def _qk_prep_kernel(q_ref, kv_ref, gq_ref, gk_ref, cos_ref, sin_ref, bd_ref,
                    qr_ref, kn_ref, khm_ref, vthm_ref):
    cos = cos_ref[...]
    sin = sin_ref[...]
    bd = bd_ref[...]
    lane = lax.broadcasted_iota(jnp.int32, (TM, W_KV), 1)
    first_half = jnp.bitwise_and(lane, HEAD_DIM // 2 - 1) < HEAD_DIM // 4
    quarter = HEAD_DIM // 4

    def norm(x, g):
        hi, lo = _split_bf16(x * x)
        ss = _mm(hi, bd) + _mm(lo, bd)
        return x * lax.rsqrt(ss * (1.0 / HEAD_DIM) + EPS) * g

    def rope(x):
        swapped = jnp.where(first_half, pltpu.roll(x, W_KV - quarter, 1), pltpu.roll(x, quarter, 1))
        return x * cos + swapped * sin

    for j in range(W_ATTN // W_KV):
        cs = slice(j * W_KV, (j + 1) * W_KV)
        qn = norm(q_ref[:, cs], gq_ref[...])
        qr_ref[:, cs] = (rope(qn) * _QK_SCALE).astype(bf16)
    kn = norm(kv_ref[:, :W_KV], gk_ref[...])
    kn_ref[...] = kn
    kr = rope(kn)
    v = kv_ref[:, W_KV:]
    for g in range(N_KV_HEADS):
        hs = slice(g * HEAD_DIM, (g + 1) * HEAD_DIM)
        khm_ref[g] = kr[:, hs].astype(bf16)
        vthm_ref[g] = v[:, hs].T.astype(bf16)


def _qk_prep_call(q, kv, q_norm, k_norm, cos_t, sin_t, bd):
    rep = W_KV // HEAD_DIM
    const = lambda shape: pl.BlockSpec(shape, lambda i: (0, 0))
    tab = pl.BlockSpec((TM, W_KV), lambda i: (jnp.where(i < CTX_TILES, 0, 1 + lax.rem(i, LAT_TPS)), 0))
    tile = lambda w: pl.BlockSpec((TM, w), lambda i: (i, 0))
    return pl.pallas_call(
        _qk_prep_kernel,
        grid=(N_TILES,),
        in_specs=[tile(W_ATTN), tile(2 * W_KV), const((1, W_KV)), const((1, W_KV)), tab, tab,
                  const((W_KV, W_KV))],
        out_specs=[tile(W_ATTN), tile(W_KV),
                   pl.BlockSpec((N_KV_HEADS, TM, HEAD_DIM), lambda i: (0, i, 0)),
                   pl.BlockSpec((N_KV_HEADS, HEAD_DIM, TM), lambda i: (0, 0, i))],
        out_shape=[jax.ShapeDtypeStruct((N_TOK, W_ATTN), bf16), jax.ShapeDtypeStruct((N_TOK, W_KV), f32),
                   jax.ShapeDtypeStruct((N_KV_HEADS, N_TOK, HEAD_DIM), bf16),
                   jax.ShapeDtypeStruct((N_KV_HEADS, HEAD_DIM, N_TOK), bf16)],
        compiler_params=_params(1),
        name="qk_prep",
    )(q, kv, jnp.tile(q_norm, rep).reshape(1, W_KV), jnp.tile(k_norm, rep).reshape(1, W_KV),
      cos_t, sin_t, bd)


_NT_DIMS = (((1,), (1,)), ((), ()))


_ATT_CK = 256


def _attn_kernel(hps, has_cache, seq_len, q_ref, k_ref, vt_ref, *rest):
    if has_cache:
        kc_ref, vct_ref, o_ref, s_ref = rest
    else:
        o_ref, s_ref = rest
    new_chunks = seq_len // _ATT_CK
    chunks = ([None] if has_cache else []) + list(range(new_chunks))
    tq = q_ref.shape[0]

    def k_chunk(g, c):
        return kc_ref[g] if c is None else k_ref[g, c * _ATT_CK:(c + 1) * _ATT_CK, :]

    def vt_chunk(g, c):
        return vct_ref[g] if c is None else vt_ref[g, :, c * _ATT_CK:(c + 1) * _ATT_CK]

    def rows(i):
        return slice(i * _ATT_CK, (i + 1) * _ATT_CK)

    def score_chunk(h, i, m):
        qh = q_ref[:, h * HEAD_DIM:(h + 1) * HEAD_DIM]
        s = lax.dot_general(k_chunk(h // Q_PER_KV, chunks[i]), qh, _NT_DIMS, preferred_element_type=f32)
        s_ref[h % 2, rows(i), :] = s
        return jnp.maximum(m, jnp.max(s, axis=0, keepdims=True))

    neg = jnp.full((1, tq), -jnp.inf, f32)
    m = neg
    for i in range(len(chunks)):
        m = score_chunk(0, i, m)
    for h in range(hps):
        m_next = neg
        l = jnp.zeros((1, tq), f32)
        acc = jnp.zeros((HEAD_DIM, tq), f32)
        for i in range(len(chunks)):
            if h + 1 < hps:
                m_next = score_chunk(h + 1, i, m_next)
            p = jnp.exp2(s_ref[h % 2, rows(i), :] - m)
            l = l + jnp.sum(p, axis=0, keepdims=True)
            acc = acc + _mm(vt_chunk(h // Q_PER_KV, chunks[i]), p.astype(bf16))
        o_ref[:, h * HEAD_DIM:(h + 1) * HEAD_DIM] = (acc / l).T.astype(bf16)
        m = m_next


def _attn_call(qr, khm, vthm, n_seq, seq_len, hps, tq, tok0, cache=None):
    kvb = hps // Q_PER_KV
    qtiles = seq_len // tq
    row0 = tok0 // tq
    seq0 = tok0 // seq_len
    in_specs = [
        pl.BlockSpec((tq, hps * HEAD_DIM), lambda b, g, t: (row0 + b * qtiles + t, g)),
        pl.BlockSpec((kvb, seq_len, HEAD_DIM), lambda b, g, t: (g, seq0 + b, 0)),
        pl.BlockSpec((kvb, HEAD_DIM, seq_len), lambda b, g, t: (g, 0, seq0 + b)),
    ]
    args = [qr, khm, vthm]
    if cache is not None:
        in_specs += [pl.BlockSpec((None, kvb, PAST_LEN, HEAD_DIM), lambda b, g, t: (b, g, 0, 0)),
                     pl.BlockSpec((None, kvb, HEAD_DIM, PAST_LEN), lambda b, g, t: (b, g, 0, 0))]
        args += list(cache)
    return pl.pallas_call(
        functools.partial(_attn_kernel, hps, cache is not None, seq_len),
        grid=(n_seq, N_Q_HEADS // hps, qtiles),
        in_specs=in_specs,
        out_specs=pl.BlockSpec((tq, hps * HEAD_DIM), lambda b, g, t: (b * qtiles + t, g)),
        out_shape=jax.ShapeDtypeStruct((n_seq * seq_len, W_ATTN), bf16),
        scratch_shapes=[pltpu.VMEM((2, seq_len + (PAST_LEN if cache is not None else 0), tq), f32)],
        compiler_params=_params(3, VMEM_LIMIT),
        name="attention",
    )(*args)


def _merge_kernel(x_ref, mod_ref, pa_ref, pb_ref, pc_ref, attc_ref, attl_ref, sa_ref, sc_ref, sd_ref, gm_ref,
                  woa_ref, wob_ref, woc_ref, wod_ref, wout_ref, o_ref):
    def gated(p_ref, s_ref):
        return (p_ref[...].astype(f32) * s_ref[...].astype(f32)).astype(bf16)

    att = jnp.where(pl.program_id(0) < CTX_TILES, attc_ref[...], attl_ref[...])

    def gm(j):
        return gm_ref[:, j * D_MODEL:(j + 1) * D_MODEL].astype(f32)

    merged = gm(0) * _mm(gated(pa_ref, sa_ref), woa_ref[...])
    merged = merged + gm(1) * _mm(pb_ref[...], wob_ref[...])
    merged = merged + gm(2) * _mm(gated(pc_ref, sc_ref), woc_ref[...])
    merged = merged + gm(3) * _mm((att.astype(f32) * sd_ref[...].astype(f32)).astype(bf16), wod_ref[...])
    y = _mm(merged.astype(bf16), wout_ref[...])
    o_ref[...] = x_ref[...] + mod_ref[2:3, :] * y


def _merge_call(x, mod_tiles_l, pa, pb, pc, att_ctx, att_lat, sa, sc, sd, gm, woa, wob, woc, wod, wout):
    tile = lambda w: pl.BlockSpec((TM, w), lambda i: (i, 0))
    const = lambda shape: pl.BlockSpec(shape, lambda i: (0, 0))
    return pl.pallas_call(
        _merge_kernel,
        grid=(N_TILES,),
        in_specs=[tile(D_MODEL), pl.BlockSpec((None, 3, D_MODEL), lambda i: (i, 0, 0)),
                  tile(W_BR), tile(W_BR), tile(W_BR),
                  pl.BlockSpec((TM, W_ATTN), lambda i: (jnp.minimum(i, CTX_TILES - 1), 0)),
                  pl.BlockSpec((TM, W_ATTN), lambda i: (jnp.maximum(i - CTX_TILES, 0), 0)),
                  tile(W_BR), tile(W_BR), tile(W_ATTN), tile(4 * D_MODEL),
                  const((W_BR, D_MODEL)), const((W_BR, D_MODEL)), const((W_BR, D_MODEL)),
                  const((W_ATTN, D_MODEL)), const((D_MODEL, D_MODEL))],
        out_specs=tile(D_MODEL),
        out_shape=jax.ShapeDtypeStruct((N_TOK, D_MODEL), f32),
        compiler_params=_params(1, VMEM_LIMIT),
        name="merge_out",
    )(x, mod_tiles_l, pa, pb, pc, att_ctx, att_lat, sa, sc, sd, gm, woa, wob, woc, wod, wout)


def kernel(x_prompt, x_sample, cache_k, cache_v, c, c_ctx, w_ada, b_ada, norm_g, w_in, conv_dw_w, conv_dw_b, conv_ln_g, conv_ln_b, conv_pw, hy_short_w, hy_short_b, hy_w1, hy_b1, hy_freq, hy_w2, hy_b2, hy_w3, hy_b3, hy_skip, pool_w, pool_scale, q_norm, k_norm, wo_conv, wo_hyena, wo_pool, wo_attn, w_out):
    x = jnp.concatenate([x_prompt.reshape(N_CTX_TOK, D_MODEL), x_sample.reshape(N_LAT_TOK, D_MODEL)], axis=0)

    cond8 = jnp.concatenate([c_ctx[None, :], c, jnp.zeros((8 - 1 - DEC_BATCH, D_MODEL), f32)], axis=0)
    mod = _ada_call(cond8, w_ada, b_ada)
    tile_cond = np.concatenate([np.zeros(CTX_TILES, np.int32),
                                1 + np.arange(N_TILES - CTX_TILES, dtype=np.int32) // LAT_TPS])
    mod_tiles = mod[:, tile_cond].reshape(DEPTH, N_TILES, 3, D_MODEL)

    groups = ((BATCH // 2, SEQ, 0), (DEC_BATCH // 2, DEC_SEQ, N_CTX_TOK // DEC_SEQ))
    tabs = {L: _dft_tables(L) for _, L, _ in groups}
    feats = {L: _filter_features(L) for _, L, _ in groups}
    absd = _abs_deltas()
    cos_t, sin_t = _rope_tables()
    bd = _group_sum_matrix()
    w1p = jnp.pad(hy_w1, ((0, 0), (0, _FEAT_PAD - FILTER_EMB), (0, 0)))

    ks, vs = [], []
    for l in range(DEPTH):
        (a, sa, bp, sb, ci, sc, q, kv, sd, gm) = _inproj_call(x, mod_tiles[l], norm_g[l], w_in[l].astype(bf16))

        pa = _conformer_call(a, conv_dw_w[l], conv_dw_b[l], conv_ln_g[l], conv_ln_b[l], conv_pw[l].astype(bf16))
        pc = _pool_call(ci, pool_w[l].astype(bf16), pool_scale[l])

        v, x1, x2 = _shortconv_call(bp, hy_short_w[l], hy_short_b[l])
        pb_parts = []
        for n_pairs, L, seq0 in groups:
            hfilt, ss = _filter_mlp_call(L, feats[L][0], feats[L][1], w1p[l], hy_b1[l], hy_freq[l], hy_w2[l],
                                         hy_b2[l], hy_w3[l], hy_b3[l], absd)
            spec = _filter_spectrum(L, tabs[L][0], hfilt, ss)
            pb_parts.append(_hyena_group(n_pairs, 2 * L // DFT_N1, seq0, tabs[L], spec, v, x1, x2, sb,
                                         hy_skip[l]))
        pb = jnp.concatenate(pb_parts, axis=0)

        qr, kn, khm, vthm = _qk_prep_call(q, kv, q_norm[l], k_norm[l], cos_t, sin_t, bd)
        att_ctx = _attn_call(qr, khm, vthm, BATCH, SEQ, N_Q_HEADS, SEQ, 0)
        cache = (cache_k[:, l].astype(bf16).transpose(0, 2, 1, 3), cache_v[:, l].astype(bf16).transpose(0, 2, 3, 1))
        att_lat = _attn_call(qr, khm, vthm, DEC_BATCH, DEC_SEQ, Q_PER_KV, 256, N_CTX_TOK, cache)

        x = _merge_call(x, mod_tiles[l], pa, pb, pc, att_ctx, att_lat, sa, sc, sd, gm,
                        wo_conv[l].astype(bf16), wo_hyena[l].astype(bf16), wo_pool[l].astype(bf16),
                        wo_attn[l].astype(bf16), w_out[l].astype(bf16))

        ks.append(kn[:N_CTX_TOK].reshape(BATCH, SEQ, N_KV_HEADS, HEAD_DIM))
        vs.append(kv[:N_CTX_TOK, W_KV:].reshape(BATCH, SEQ, N_KV_HEADS, HEAD_DIM))

    y_prompt = x[:N_CTX_TOK].reshape(BATCH, SEQ, D_MODEL)
    y_sample = x[N_CTX_TOK:].reshape(DEC_BATCH, DEC_SEQ, D_MODEL)
    return (y_prompt, y_sample, jnp.stack(ks, axis=1), jnp.stack(vs, axis=1))
```

```python
import functools
import math

import numpy as np
import jax
import jax.numpy as jnp
from jax import lax
from jax.experimental import pallas as pl
from jax.experimental.pallas import tpu as pltpu

f32 = jnp.float32
bf16 = jnp.bfloat16

D_MODEL = 1024
BATCH = 16
SEQ = 256
DEPTH = 2
DEC_BATCH = 2
DEC_SEQ = 4096
PAST_LEN = 256
GRID_W = 64
W_BR = 512
CONV_K = 31
SHORT_K = 3
FILTER_BANDS = 16
FILTER_EMB = 1 + 2 * FILTER_BANDS
FILTER_HIDDEN = 64
DECAY_TARGET = 1e-2
FAST_DECAY_PCT = 0.3
SLOW_DECAY_PCT = 1.5
POOL_WINDOWS = (2, 4, 8, 16)
POOL_GROUP = W_BR // len(POOL_WINDOWS)
HEAD_DIM = 64
N_Q_HEADS = 16
N_KV_HEADS = 4
Q_PER_KV = N_Q_HEADS // N_KV_HEADS
W_ATTN = N_Q_HEADS * HEAD_DIM
W_KV = N_KV_HEADS * HEAD_DIM
ROPE_THETA = 10000.0
EPS = 1e-6
N_IN = 2 * W_BR + W_BR + 3 * W_BR + W_BR + W_BR + W_BR + W_ATTN + 2 * W_KV + W_ATTN + 4 * D_MODEL

TM = 256
N_CTX_TOK = BATCH * SEQ
N_LAT_TOK = DEC_BATCH * DEC_SEQ
N_TOK = N_CTX_TOK + N_LAT_TOK
N_TILES = N_TOK // TM
CTX_TILES = N_CTX_TOK // TM
LAT_TPS = DEC_SEQ // TM
DFT_B = 256
LANES = 128
SUBLANES = 8
VMEM_LIMIT = 56 * 1024 * 1024

assert SEQ == TM == DFT_B and CTX_TILES % LAT_TPS == 0


def _sigmoid(x):
    return 1.0 / (1.0 + jnp.exp(-x))


def _silu(x):
    return x * _sigmoid(x)


def _mm(a, b):
    return jnp.dot(a, b, preferred_element_type=f32)


def _split_bf16(a):
    hi = a.astype(bf16)
    lo = (a - hi.astype(f32)).astype(bf16)
    return hi, lo


def _mm3(a, b):
    ah, al = _split_bf16(a)
    bh, bl = _split_bf16(b)
    return _mm(ah, bh) + (_mm(ah, bl) + _mm(al, bh))


def _params(n_axes, vmem=None):
    return pltpu.CompilerParams(dimension_semantics=("arbitrary",) * n_axes,
                                vmem_limit_bytes=vmem)


def _tile_position():
    i = pl.program_id(0)
    return i >= CTX_TILES, lax.rem(i, LAT_TPS)


def _halo_flags():
    is_lat, jt = _tile_position()
    return jnp.logical_and(is_lat, jt > 0), jnp.logical_and(is_lat, jt < LAT_TPS - 1)


def _halo_specs(width, hb):
    r = TM // hb
    last = N_TOK // hb - 1
    return [
        pl.BlockSpec((hb, width), lambda i: (jnp.maximum(i * r - 1, 0), 0)),
        pl.BlockSpec((TM, width), lambda i: (i, 0)),
        pl.BlockSpec((hb, width), lambda i: (jnp.minimum((i + 1) * r, last), 0)),
    ]


def _ada_kernel(c_ref, w_ref, b_ref, o_ref):
    c = c_ref[...]
    o_ref[...] = _mm(_silu(c).astype(bf16), w_ref[...].astype(bf16)) + b_ref[...]


def _ada_call(cond8, w_ada, b_ada):
    nb = 3 * D_MODEL // D_MODEL
    return pl.pallas_call(
        _ada_kernel,
        grid=(DEPTH, nb),
        in_specs=[
            pl.BlockSpec((8, D_MODEL), lambda l, j: (0, 0)),
            pl.BlockSpec((None, D_MODEL, D_MODEL), lambda l, j: (l, 0, j)),
            pl.BlockSpec((None, 1, D_MODEL), lambda l, j: (l, 0, j)),
        ],
        out_specs=pl.BlockSpec((None, 8, D_MODEL), lambda l, j: (l, 0, j)),
        out_shape=jax.ShapeDtypeStruct((DEPTH, 8, 3 * D_MODEL), f32),
        compiler_params=_params(2, VMEM_LIMIT),
        name="ada_mod",
    )(cond8, w_ada, b_ada.reshape(DEPTH, 1, 3 * D_MODEL))


_C_GLU = 0
_C_AGATE = 2 * W_BR
_C_BPROJ = _C_AGATE + W_BR
_C_BGATE = _C_BPROJ + 3 * W_BR
_C_CIN = _C_BGATE + W_BR
_C_CGATE = _C_CIN + W_BR
_C_Q = _C_CGATE + W_BR
_C_KV = _C_Q + W_ATTN
_C_DGATE = _C_KV + 2 * W_KV
_C_GM = _C_DGATE + W_ATTN
assert _C_GM + 4 * D_MODEL == N_IN


def _inproj_kernel(x_ref, mod_ref, g_ref, w_ref,
                   a_ref, sa_ref, bp_ref, sb_ref, ci_ref, sc_ref, q_ref, kv_ref, sd_ref, gm_ref):
    x = x_ref[...]
    ms = jnp.mean(x * x, axis=-1, keepdims=True)
    hn = x * lax.rsqrt(ms + EPS) * g_ref[...]
    h = (hn * (1.0 + mod_ref[1:2, :]) + mod_ref[0:1, :]).astype(bf16)

    def proj(c0, width):
        return _mm(h, w_ref[:, c0:c0 + width])

    a_ref[...] = proj(_C_GLU, W_BR) * _sigmoid(proj(_C_GLU + W_BR, W_BR))
    sa_ref[...] = _silu(proj(_C_AGATE, W_BR)).astype(bf16)
    for j in range(3):
        bp_ref[:, j * W_BR:(j + 1) * W_BR] = proj(_C_BPROJ + j * W_BR, W_BR)
    sb_ref[...] = _silu(proj(_C_BGATE, W_BR)).astype(bf16)
    ci_ref[...] = proj(_C_CIN, W_BR)
    sc_ref[...] = _silu(proj(_C_CGATE, W_BR)).astype(bf16)
    for j in range(W_ATTN // W_BR):
        q_ref[:, j * W_BR:(j + 1) * W_BR] = proj(_C_Q + j * W_BR, W_BR)
    kv_ref[...] = proj(_C_KV, 2 * W_KV)
    for j in range(W_ATTN // W_BR):
        sd_ref[:, j * W_BR:(j + 1) * W_BR] = _silu(proj(_C_DGATE + j * W_BR, W_BR)).astype(bf16)
    for j in range(4 * D_MODEL // W_BR):
        gm_ref[:, j * W_BR:(j + 1) * W_BR] = _sigmoid(proj(_C_GM + j * W_BR, W_BR)).astype(bf16)


def _inproj_call(x, mod_tiles_l, norm_g_l, w_in_l):
    widths = [(W_BR, f32), (W_BR, bf16), (3 * W_BR, f32), (W_BR, bf16), (W_BR, f32), (W_BR, bf16),
              (W_ATTN, f32), (2 * W_KV, f32), (W_ATTN, bf16), (4 * D_MODEL, bf16)]
    return pl.pallas_call(
        _inproj_kernel,
        grid=(N_TILES,),
        in_specs=[
            pl.BlockSpec((TM, D_MODEL), lambda i: (i, 0)),
            pl.BlockSpec((None, 3, D_MODEL), lambda i: (i, 0, 0)),
            pl.BlockSpec((1, D_MODEL), lambda i: (0, 0)),
            pl.BlockSpec((D_MODEL, N_IN), lambda i: (0, 0), pipeline_mode=pl.Buffered(1)),
        ],
        out_specs=[pl.BlockSpec((TM, w), lambda i: (i, 0)) for w, _ in widths],
        out_shape=[jax.ShapeDtypeStruct((N_TOK, w), dt) for w, dt in widths],
        compiler_params=_params(1, VMEM_LIMIT),
        name="in_proj",
    )(x, mod_tiles_l, norm_g_l.reshape(1, D_MODEL), w_in_l)


_CONF_HB = 16
_CONF_CHUNK = 32
_CONF_SPAN = TM + 2 * _CONF_HB - SUBLANES
assert (CONV_K - 1 + _CONF_HB - CONV_K // 2) // SUBLANES * SUBLANES + TM <= _CONF_SPAN


def _conformer_kernel(prev_ref, cur_ref, next_ref, dw_ref, dwb_ref, lng_ref, lnb_ref, pw_ref,
                      o_ref, pad_ref, shift_ref, conv_ref):
    has_prev, has_next = _halo_flags()
    pad_ref[0:_CONF_HB, :] = jnp.where(has_prev, prev_ref[...], 0.0)
    pad_ref[_CONF_HB:_CONF_HB + TM, :] = cur_ref[...]
    pad_ref[_CONF_HB + TM:, :] = jnp.where(has_next, next_ref[...], 0.0)
    for r in range(SUBLANES):
        shift_ref[r] = pad_ref[r:r + _CONF_SPAN, :]
    off = _CONF_HB - CONV_K // 2
    for c in range(TM // _CONF_CHUNK):
        r0 = c * _CONF_CHUNK
        groups = _CONF_CHUNK // SUBLANES
        acc = jnp.broadcast_to(dwb_ref[...][None], (groups, SUBLANES, W_BR))
        for j in range(CONV_K):
            a0 = r0 + (j + off) // SUBLANES * SUBLANES
            x = shift_ref[(j + off) % SUBLANES, a0:a0 + _CONF_CHUNK, :]
            acc = acc + dw_ref[j][None] * x.reshape(groups, SUBLANES, W_BR)
        conv_ref[r0:r0 + _CONF_CHUNK, :] = acc.reshape(_CONF_CHUNK, W_BR)
    a = conv_ref[...]
    mu = jnp.mean(a, axis=-1, keepdims=True)
    xc = a - mu
    var = jnp.mean(xc * xc, axis=-1, keepdims=True)
    y = xc * lax.rsqrt(var + EPS) * lng_ref[...] + lnb_ref[...]
    o_ref[...] = _mm(_silu(y).astype(bf16), pw_ref[...]).astype(bf16)


def _conformer_call(a, dw_w, dw_b, ln_g, ln_b, pw_bf):
    row = lambda v: v.reshape(1, W_BR)
    rep = lambda v: jnp.broadcast_to(v[..., None, :], v.shape[:-1] + (SUBLANES, W_BR))
    const = lambda shape: pl.BlockSpec(shape, lambda i: (0,) * len(shape))
    return pl.pallas_call(
        _conformer_kernel,
        grid=(N_TILES,),
        in_specs=_halo_specs(W_BR, _CONF_HB) + [
            const((CONV_K, SUBLANES, W_BR)), const((SUBLANES, W_BR)), const((1, W_BR)), const((1, W_BR)),
            const((W_BR, W_BR))],
        out_specs=pl.BlockSpec((TM, W_BR), lambda i: (i, 0)),
        out_shape=jax.ShapeDtypeStruct((N_TOK, W_BR), bf16),
        scratch_shapes=[pltpu.VMEM((TM + 2 * _CONF_HB, W_BR), f32),
                        pltpu.VMEM((SUBLANES, _CONF_SPAN, W_BR), f32), pltpu.VMEM((TM, W_BR), f32)],
        compiler_params=_params(1, VMEM_LIMIT),
        name="conformer",
    )(a, a, a, rep(dw_w), rep(dw_b), row(ln_g), row(ln_b), pw_bf)


_HB8 = 8


def _shortconv_kernel(prev_ref, cur_ref, next_ref, w_ref, b_ref, v_ref, x1_ref, x2_ref, pad_ref):
    has_prev, has_next = _halo_flags()
    pad_ref[0:_HB8, :] = jnp.where(has_prev, prev_ref[...], 0.0)
    pad_ref[_HB8:_HB8 + TM, :] = cur_ref[...]
    pad_ref[_HB8 + TM:, :] = jnp.where(has_next, next_ref[...], 0.0)
    outs = (v_ref, x1_ref, x2_ref)
    for g in range(3):
        cs = slice(g * W_BR, (g + 1) * W_BR)
        acc = jnp.broadcast_to(b_ref[:, cs], (TM, W_BR))
        for j in range(SHORT_K):
            r0 = _HB8 + j - SHORT_K // 2
            acc = acc + w_ref[j:j + 1, cs] * pad_ref[r0:r0 + TM, cs]
        outs[g][...] = acc


def _shortconv_call(bp, w, b):
    width = 3 * W_BR
    const = lambda shape: pl.BlockSpec(shape, lambda i: (0, 0))
    out = jax.ShapeDtypeStruct((N_TOK, W_BR), f32)
    return pl.pallas_call(
        _shortconv_kernel,
        grid=(N_TILES,),
        in_specs=_halo_specs(width, _HB8) + [const((SHORT_K, width)), const((1, width))],
        out_specs=[pl.BlockSpec((TM, W_BR), lambda i: (i, 0))] * 3,
        out_shape=[out, out, out],
        scratch_shapes=[pltpu.VMEM((TM + 2 * _HB8, width), f32)],
        compiler_params=_params(1),
        name="hyena_short",
    )(bp, bp, bp, w, b.reshape(1, width))


def _pool_kernel(prev_ref, cur_ref, next_ref, pw_ref, ps_ref, o_ref, pad_ref):
    has_prev, has_next = _halo_flags()
    is_lat, jt = _tile_position()
    pad_ref[0:_HB8, :] = jnp.where(has_prev, prev_ref[...], 0.0)
    pad_ref[_HB8:_HB8 + TM, :] = cur_ref[...]
    pad_ref[_HB8 + TM:, :] = jnp.where(has_next, next_ref[...], 0.0)
    seq_len = jnp.where(is_lat, DEC_SEQ, SEQ)
    t = jnp.where(is_lat, jt * TM, 0) + lax.broadcasted_iota(jnp.int32, (TM, POOL_GROUP), 0)
    for g, win in enumerate(POOL_WINDOWS):
        cs = slice(g * POOL_GROUP, (g + 1) * POOL_GROUP)
        hw = win // 2
        s = pad_ref[_HB8 - hw:_HB8 - hw + TM, cs]
        for d in range(-hw + 1, hw):
            s = s + pad_ref[_HB8 + d:_HB8 + d + TM, cs]
        cnt = (jnp.minimum(t + hw, seq_len) - jnp.maximum(t - hw, 0)).astype(f32)
        pooled = s / cnt - cur_ref[:, cs]
        y = _mm(pooled.astype(bf16), pw_ref[g])
        o_ref[:, cs] = (y * ps_ref[:, cs]).astype(bf16)


def _pool_call(ci, pool_w_bf, pool_scale):
    return pl.pallas_call(
        _pool_kernel,
        grid=(N_TILES,),
        in_specs=_halo_specs(W_BR, _HB8) + [
            pl.BlockSpec((len(POOL_WINDOWS), POOL_GROUP, POOL_GROUP), lambda i: (0, 0, 0)),
            pl.BlockSpec((1, W_BR), lambda i: (0, 0))],
        out_specs=pl.BlockSpec((TM, W_BR), lambda i: (i, 0)),
        out_shape=jax.ShapeDtypeStruct((N_TOK, W_BR), bf16),
        scratch_shapes=[pltpu.VMEM((TM + 2 * _HB8, W_BR), f32)],
        compiler_params=_params(1),
        name="pool",
    )(ci, ci, ci, pool_w_bf, pool_scale.reshape(1, W_BR))


def _dft_tables(n2):
    n = DFT_B * n2
    k1 = np.arange(DFT_B, dtype=np.int64)[:, None]
    t1 = np.arange(DFT_B, dtype=np.int64)[None, :]
    fwd, inv = [], []
    for k2 in range(n2):
        ang = -2.0 * np.pi * (((n2 * k1 + k2) * t1) % n).astype(np.float64) / n
        gr, gi = np.cos(ang), np.sin(ang)
        fwd.append(np.concatenate([gr, gi], axis=0))
        inv.append(np.concatenate([gr.T, gi.T], axis=1))
    return (jnp.asarray(np.stack(fwd).astype(np.float32)).astype(bf16),
            jnp.asarray(np.stack(inv).astype(np.float32)).astype(bf16))


def _cmul_const(x, ang):
    c, s = math.cos(ang), math.sin(ang)
    xr, xi = x
    return (c * xr - s * xi, s * xr + c * xi)


def _fft_list(xs, sign):
    n = len(xs)
    if n == 1:
        return list(xs)
    ev = _fft_list(xs[0::2], sign)
    od = _fft_list(xs[1::2], sign)
    out = [None] * n
    for k in range(n // 2):
        orr, oi = od[k]
        if k == 0:
            tr, ti = orr, oi
        elif 4 * k == n:
            tr, ti = (-oi, orr) if sign > 0 else (oi, -orr)
        else:
            tr, ti = _cmul_const(od[k], sign * 2.0 * math.pi * k / n)
        er, ei = ev[k]
        out[k] = (er + tr, ei + ti)
        out[k + n // 2] = (er - tr, ei - ti)
    return out


def _fft_padded(xs, sign):
    m = len(xs)
    ev = _fft_list(xs, sign)
    od = _fft_list([xs[0]] + [_cmul_const(xs[t], sign * math.pi * t / m) for t in range(1, m)], sign)
    out = []
    for j in range(m):
        out += [ev[j], od[j]]
    return out


def _ifft_truncated(ws):
    m = len(ws) // 2
    ev = _fft_list(ws[0::2], 1)
    od = _fft_list(ws[1::2], 1)
    out = []
    for t in range(m):
        tr, ti = od[t] if t == 0 else _cmul_const(od[t], math.pi * t / m)
        out.append((ev[t][0] + tr, ev[t][1] + ti))
    return out


def _spectral_block(u, m, mt, hr, hi):
    w = hr.shape[1]
    p = _mm(m, u)
    xr = p[:DFT_B, :w] - p[DFT_B:, w:]
    xi = p[:DFT_B, w:] + p[DFT_B:, :w]
    yr = xr * hr - xi * hi
    yi = xr * hi + xi * hr
    rhs = jnp.concatenate([jnp.concatenate([yr, yi], axis=1),
                           jnp.concatenate([yi, -yr], axis=1)], axis=0).astype(bf16)
    return _mm(mt, rhs)


def _slab(s):
    return pl.ds(pl.multiple_of(s * SUBLANES, SUBLANES), SUBLANES)


_FEAT_PAD = 64


def _filter_features(seq_len):
    m = np.arange(2 * seq_len)
    j = np.where(m < seq_len, m, 2 * seq_len - m).astype(np.float64)
    t = j / (seq_len - 1)
    bands = np.linspace(1e-4, FILTER_BANDS - 1, FILTER_BANDS)[None, :]
    w = (2.0 * math.pi / seq_len) * j[:, None]
    z = np.concatenate([t[:, None], np.cos(bands * w), np.sin(bands * w)], axis=-1)
    z = np.pad(z, ((0, 0), (0, _FEAT_PAD - FILTER_EMB)))
    return jnp.asarray(z.astype(np.float32)), jnp.asarray(t.astype(np.float32)[:, None])


def _abs_deltas():
    max_decay = math.log(DECAY_TARGET) / FAST_DECAY_PCT
    min_decay = math.log(DECAY_TARGET) / SLOW_DECAY_PCT
    d = np.abs(np.linspace(min_decay, max_decay, W_BR))
    return jnp.asarray(np.concatenate([d, d]).astype(np.float32)[None, :])


def _filter_mlp_kernel(seq_len, tr, z_ref, t_ref, w1_ref, b1_ref, fr_ref, w2_ref, b2_ref, w3_ref, b3_ref,
                       ad_ref, h_ref, ss_ref):
    i = pl.program_id(0)
    fr = fr_ref[...]
    hdn = jnp.sin(fr * (_mm3(z_ref[...], w1_ref[...]) + b1_ref[...]))
    hdn = jnp.sin(fr * (_mm3(hdn, w2_ref[...]) + b2_ref[...]))
    h = _mm3(hdn, w3_ref[...]) + b3_ref[...]
    h = h * jnp.exp(-t_ref[...] * ad_ref[...])
    m = i * tr + lax.broadcasted_iota(jnp.int32, h.shape, 0)
    h = jnp.where(m == seq_len, 0.0, h)
    h_ref[...] = h

    @pl.when(i == 0)
    def _():
        ss_ref[...] = jnp.zeros_like(ss_ref)

    ss_ref[...] += jnp.sum(h * h, axis=0, keepdims=True)


def _filter_mlp_call(seq_len, feats, tcol, w1p, b1, freq, w2, b2, w3, b3, absd):
    tr = min(512, seq_len)
    steps = 2 * seq_len // tr
    half_steps = seq_len // tr
    wide = 2 * W_BR
    const = lambda shape: pl.BlockSpec(shape, lambda i: (0, 0))
    row = lambda v: v.reshape(1, -1)
    return pl.pallas_call(
        functools.partial(_filter_mlp_kernel, seq_len, tr),
        grid=(steps,),
        in_specs=[
            pl.BlockSpec((tr, _FEAT_PAD), lambda i: (i, 0)),
            pl.BlockSpec((tr, 1), lambda i: (i, 0)),
            const((_FEAT_PAD, FILTER_HIDDEN)), const((1, FILTER_HIDDEN)), const((1, FILTER_HIDDEN)),
            const((FILTER_HIDDEN, FILTER_HIDDEN)), const((1, FILTER_HIDDEN)),
            pl.BlockSpec((FILTER_HIDDEN, wide), lambda i: (0, i // half_steps)),
            pl.BlockSpec((1, wide), lambda i: (0, i // half_steps)),
            const((1, wide)),
        ],
        out_specs=[pl.BlockSpec((tr, wide), lambda i: (i, 0)), const((1, wide))],
        out_shape=[jax.ShapeDtypeStruct((2 * seq_len, wide), f32), jax.ShapeDtypeStruct((1, wide), f32)],
        compiler_params=_params(1),
        name="filter_mlp",
    )(feats, tcol, w1p, row(b1), row(freq), w2, row(b2), w3, row(b3), absd)


_RADIX_ROWS = 128


def _filter_radix_kernel(n2, h_ref, ur_ref, ui_ref, sr_ref, si_ref):
    zero = jnp.zeros((SUBLANES, LANES), f32)

    def body(s, carry):
        rs = _slab(s)
        us = _fft_list([(h_ref[t, rs, :], zero) for t in range(n2)], -1)
        for k in range(n2):
            sr_ref[k, rs, :] = us[k][0]
            si_ref[k, rs, :] = us[k][1]
        return carry

    lax.fori_loop(0, _RADIX_ROWS // SUBLANES, body, 0)
    for k in range(n2):
        ur_ref[k] = sr_ref[k].astype(bf16)
        ui_ref[k] = si_ref[k].astype(bf16)


def _filter_spec_kernel(scale_const, ur_ref, ui_ref, m_ref, ss_ref, hr_ref, hi_ref):
    w = ur_ref.shape[1]
    p = _mm(m_ref[...], jnp.concatenate([ur_ref[...], ui_ref[...]], axis=1))
    scale = lax.rsqrt(ss_ref[...] + EPS) * scale_const
    hr_ref[...] = (p[:DFT_B, :w] - p[DFT_B:, w:]) * scale
    hi_ref[...] = (p[:DFT_B, w:] + p[DFT_B:, :w]) * scale


def _filter_spectrum(n2, mfwd, hfilt, ss):
    wide = 2 * W_BR
    blk = pl.BlockSpec((n2, _RADIX_ROWS, LANES), lambda r, c: (0, r, c))
    u_shape = jax.ShapeDtypeStruct((n2, DFT_B, wide), bf16)
    ur, ui = pl.pallas_call(
        functools.partial(_filter_radix_kernel, n2),
        grid=(DFT_B // _RADIX_ROWS, wide // LANES),
        in_specs=[blk],
        out_specs=[blk, blk],
        out_shape=[u_shape, u_shape],
        scratch_shapes=[pltpu.VMEM((n2, _RADIX_ROWS, LANES), f32)] * 2,
        compiler_params=_params(2),
        name="filter_radix",
    )(hfilt.reshape(n2, DFT_B, wide))
    kspec = pl.BlockSpec((None, DFT_B, wide), lambda k: (k, 0, 0))
    h_shape = jax.ShapeDtypeStruct((n2, DFT_B, wide), f32)
    return pl.pallas_call(
        functools.partial(_filter_spec_kernel, 1.0 / (n2 * DFT_B)),
        grid=(n2,),
        in_specs=[kspec, kspec, pl.BlockSpec((None, 2 * DFT_B, DFT_B), lambda k: (k, 0, 0)),
                  pl.BlockSpec((1, wide), lambda k: (0, 0))],
        out_specs=[kspec, kspec],
        out_shape=[h_shape, h_shape],
        compiler_params=_params(1),
        name="filter_spec",
    )(ur, ui, mfwd, ss)


def _hy_ctx_kernel(va_ref, vb_ref, x1a_ref, x1b_ref, x2a_ref, x2b_ref, m_ref, mt_ref, hr_ref, hi_ref,
                   skip_ref, o_ref):
    n2 = m_ref.shape[0]
    za, zb = va_ref[...], vb_ref[...]
    gates = ((x1a_ref, x1b_ref), (x2a_ref, x2b_ref))
    for order in range(2):
        cs = slice(order * W_BR, (order + 1) * W_BR)
        u = jnp.concatenate([za, zb], axis=1).astype(bf16)
        w = _spectral_block(u, m_ref[0], mt_ref[0], hr_ref[0, :, cs], hi_ref[0, :, cs])
        for k2 in range(1, n2):
            w = w + _spectral_block(u, m_ref[k2], mt_ref[k2], hr_ref[k2, :, cs], hi_ref[k2, :, cs])
        skip = skip_ref[order:order + 1, :]
        za = gates[order][0][...] * (w[:, :W_BR] + za * skip)
        zb = gates[order][1][...] * (w[:, W_BR:] + zb * skip)
    o_ref[0:DFT_B, :] = za
    o_ref[DFT_B:, :] = zb


def _hy_ctx_call(tabs, spec, v, x1, x2, skip):
    mfwd, minv = tabs
    n2 = mfwd.shape[0]
    assert n2 == 2
    seq = lambda par: pl.BlockSpec((DFT_B, W_BR), lambda p: (2 * p + par, 0))
    whole = lambda a: pl.BlockSpec(a.shape, lambda p: (0,) * a.ndim)
    return pl.pallas_call(
        _hy_ctx_kernel,
        grid=(BATCH // 2,),
        in_specs=[seq(0), seq(1), seq(0), seq(1), seq(0), seq(1), whole(mfwd), whole(minv),
                  whole(spec[0]), whole(spec[1]), whole(skip)],
        out_specs=pl.BlockSpec((2 * DFT_B, W_BR), lambda p: (p, 0)),
        out_shape=jax.ShapeDtypeStruct((N_CTX_TOK, W_BR), f32),
        compiler_params=_params(1, VMEM_LIMIT),
        name="hyena_ctx",
    )(v, v, x1, x1, x2, x2, mfwd, minv, spec[0], spec[1], skip)


def _store_radix(us, rs, sr_ref, si_ref):
    for k, (ur, ui) in enumerate(us):
        sr_ref[k, rs, :] = ur
        si_ref[k, rs, :] = ui


def _cast_radix(sr_ref, si_ref, ur_ref, ui_ref):
    for k in range(sr_ref.shape[0]):
        ur_ref[k] = sr_ref[k].astype(bf16)
        ui_ref[k] = si_ref[k].astype(bf16)


def _hy_fwd_radix_kernel(m, za_ref, zb_ref, ur_ref, ui_ref, sr_ref, si_ref):
    def body(s, carry):
        rs = _slab(s)
        _store_radix(_fft_padded([(za_ref[t, rs, :], zb_ref[t, rs, :]) for t in range(m)], -1),
                     rs, sr_ref, si_ref)
        return carry

    lax.fori_loop(0, _RADIX_ROWS // SUBLANES, body, 0)
    _cast_radix(sr_ref, si_ref, ur_ref, ui_ref)


def _hy_spec_kernel(ur_ref, ui_ref, m_ref, mt_ref, hr_ref, hi_ref, w_ref):
    u = jnp.concatenate([ur_ref[...], ui_ref[...]], axis=1)
    w_ref[...] = _spectral_block(u, m_ref[...], mt_ref[...], hr_ref[...], hi_ref[...])


def _hy_inv_radix_kernel(m, final, wr_ref, wi_ref, za_ref, zb_ref, ga_ref, gb_ref, skip_ref, o_ref, *rest):
    skip = skip_ref[...]

    def body(s, carry):
        rs = _slab(s)
        ys = _ifft_truncated([(wr_ref[k, rs, :], wi_ref[k, rs, :]) for k in range(2 * m)])
        zs = []
        for t in range(m):
            za = ga_ref[t, rs, :] * (ys[t][0] + za_ref[t, rs, :] * skip)
            zb = gb_ref[t, rs, :] * (ys[t][1] + zb_ref[t, rs, :] * skip)
            o_ref[0, t, rs, :] = za
            o_ref[1, t, rs, :] = zb
            zs.append((za, zb))
        if not final:
            _store_radix(_fft_padded(zs, -1), rs, rest[2], rest[3])
        return carry

    lax.fori_loop(0, _RADIX_ROWS // SUBLANES, body, 0)
    if not final:
        _cast_radix(rest[2], rest[3], rest[0], rest[1])


def _hy_lat_call(tabs, spec, v, x1, x2, skip):
    mfwd, minv = tabs
    n2 = mfwd.shape[0]
    m = n2 // 2
    assert m * DFT_B == DEC_SEQ and DEC_BATCH == 2
    seq0 = N_CTX_TOK // DEC_SEQ
    grid = (DFT_B // _RADIX_ROWS, W_BR // LANES)

    def tok(arr, b):
        return (arr.reshape(N_TOK // DEC_SEQ, m, DFT_B, W_BR),
                pl.BlockSpec((None, m, _RADIX_ROWS, LANES), lambda r, c: (seq0 + b, 0, r, c)))

    def pair(arr, b):
        return (arr, pl.BlockSpec((None, m, _RADIX_ROWS, LANES), lambda r, c: (b, 0, r, c)))

    u_spec = pl.BlockSpec((n2, _RADIX_ROWS, LANES), lambda r, c: (0, r, c))
    u_shape = jax.ShapeDtypeStruct((n2, DFT_B, W_BR), bf16)
    z_spec = pl.BlockSpec((2, m, _RADIX_ROWS, LANES), lambda r, c: (0, 0, r, c))
    z_shape = jax.ShapeDtypeStruct((2, m, DFT_B, W_BR), f32)
    scratch = [pltpu.VMEM((n2, _RADIX_ROWS, LANES), f32)] * 2

    def spectral(ur, ui, order):
        kspec = pl.BlockSpec((None, DFT_B, W_BR), lambda k: (k, 0, 0))
        hspec = pl.BlockSpec((None, DFT_B, W_BR), lambda k: (k, 0, order))
        return pl.pallas_call(
            _hy_spec_kernel,
            grid=(n2,),
            in_specs=[kspec, kspec, pl.BlockSpec((None, 2 * DFT_B, DFT_B), lambda k: (k, 0, 0)),
                      pl.BlockSpec((None, DFT_B, 2 * DFT_B), lambda k: (k, 0, 0)), hspec, hspec],
            out_specs=pl.BlockSpec((None, DFT_B, 2 * W_BR), lambda k: (k, 0, 0)),
            out_shape=jax.ShapeDtypeStruct((n2, DFT_B, 2 * W_BR), f32),
            compiler_params=_params(1),
            name="hyena_spec",
        )(ur, ui, mfwd, minv, spec[0], spec[1])

    def inverse(w, z_srcs, gate, order, final):
        wr_spec = pl.BlockSpec((n2, _RADIX_ROWS, LANES), lambda r, c: (0, r, c))
        wi_spec = pl.BlockSpec((n2, _RADIX_ROWS, LANES), lambda r, c: (0, r, W_BR // LANES + c))
        g_srcs = [tok(gate, 0), tok(gate, 1)]
        return pl.pallas_call(
            functools.partial(_hy_inv_radix_kernel, m, final),
            grid=grid,
            in_specs=[wr_spec, wi_spec, z_srcs[0][1], z_srcs[1][1], g_srcs[0][1], g_srcs[1][1],
                      pl.BlockSpec((1, LANES), lambda r, c: (0, c))],
            out_specs=z_spec if final else [z_spec, u_spec, u_spec],
            out_shape=z_shape if final else [z_shape, u_shape, u_shape],
            scratch_shapes=[] if final else scratch,
            compiler_params=_params(2, VMEM_LIMIT),
            name="hyena_inv_radix",
        )(w, w, z_srcs[0][0], z_srcs[1][0], g_srcs[0][0], g_srcs[1][0], skip[order].reshape(1, W_BR))

    v_srcs = [tok(v, 0), tok(v, 1)]
    ur, ui = pl.pallas_call(
        functools.partial(_hy_fwd_radix_kernel, m),
        grid=grid,
        in_specs=[v_srcs[0][1], v_srcs[1][1]],
        out_specs=[u_spec, u_spec],
        out_shape=[u_shape, u_shape],
        scratch_shapes=scratch,
        compiler_params=_params(2),
        name="hyena_fwd_radix",
    )(v_srcs[0][0], v_srcs[1][0])
    z1, ur, ui = inverse(spectral(ur, ui, 0), v_srcs, x1, 0, False)
    return inverse(spectral(ur, ui, 1), [pair(z1, 0), pair(z1, 1)], x2, 1, True)


_QK_SCALE = HEAD_DIM ** -0.5 * math.log2(math.e)


def _rope_tables():
    half = HEAD_DIM // 2
    f = half // 2
    inv = ROPE_THETA ** (-np.arange(f, dtype=np.float64) / f)
    pos = np.arange(DEC_SEQ)
    row, col = (pos // GRID_W).astype(np.float64), (pos % GRID_W).astype(np.float64)
    ang = np.concatenate([np.tile(row[:, None] * inv[None, :], (1, 2)),
                          np.tile(col[:, None] * inv[None, :], (1, 2))], axis=1)
    sign = np.tile(np.concatenate([-np.ones(f), np.ones(f)]), 2)[None, :]
    cos = np.concatenate([np.ones((TM, HEAD_DIM)), np.cos(ang)], axis=0)
    sin = np.concatenate([np.zeros((TM, HEAD_DIM)), np.sin(ang) * sign], axis=0)
    rep = W_KV // HEAD_DIM
    return (jnp.asarray(np.tile(cos, (1, rep)).astype(np.float32)),
            jnp.asarray(np.tile(sin, (1, rep)).astype(np.float32)))


def _group_sum_matrix():
    idx = np.arange(W_KV) // HEAD_DIM
    return jnp.asarray((idx[:, None] == idx[None, :]).astype(np.float32)).astype(bf16)


def _qk_prep_kernel(q_ref, kv_ref, gq_ref, gk_ref, cos_ref, sin_ref, bd_ref,
                    qr_ref, kn_ref, khm_ref, vthm_ref):
    cos = cos_ref[...]
    sin = sin_ref[...]
    bd = bd_ref[...]
    lane = lax.broadcasted_iota(jnp.int32, (TM, W_KV), 1)
    first_half = jnp.bitwise_and(lane, HEAD_DIM // 2 - 1) < HEAD_DIM // 4
    quarter = HEAD_DIM // 4

    def norm(x, g):
        hi, lo = _split_bf16(x * x)
        ss = _mm(hi, bd) + _mm(lo, bd)
        return x * lax.rsqrt(ss * (1.0 / HEAD_DIM) + EPS) * g

    def rope(x):
        swapped = jnp.where(first_half, pltpu.roll(x, W_KV - quarter, 1), pltpu.roll(x, quarter, 1))
        return x * cos + swapped * sin

    for j in range(W_ATTN // W_KV):
        cs = slice(j * W_KV, (j + 1) * W_KV)
        qn = norm(q_ref[:, cs], gq_ref[...])
        qr_ref[:, cs] = (rope(qn) * _QK_SCALE).astype(bf16)
    kn = norm(kv_ref[:, :W_KV], gk_ref[...])
    kn_ref[...] = kn
    kr = rope(kn)
    v = kv_ref[:, W_KV:]
    for g in range(N_KV_HEADS):
        hs = slice(g * HEAD_DIM, (g + 1) * HEAD_DIM)
        khm_ref[g] = kr[:, hs].astype(bf16)
        vthm_ref[g] = v[:, hs].T.astype(bf16)


def _qk_prep_call(q, kv, q_norm, k_norm, cos_t, sin_t, bd):
    rep = W_KV // HEAD_DIM
    const = lambda shape: pl.BlockSpec(shape, lambda i: (0, 0))
    tab = pl.BlockSpec((TM, W_KV), lambda i: (jnp.where(i < CTX_TILES, 0, 1 + lax.rem(i, LAT_TPS)), 0))
    tile = lambda w: pl.BlockSpec((TM, w), lambda i: (i, 0))
    return pl.pallas_call(
        _qk_prep_kernel,
        grid=(N_TILES,),
        in_specs=[tile(W_ATTN), tile(2 * W_KV), const((1, W_KV)), const((1, W_KV)), tab, tab,
                  const((W_KV, W_KV))],
        out_specs=[tile(W_ATTN), tile(W_KV),
                   pl.BlockSpec((N_KV_HEADS, TM, HEAD_DIM), lambda i: (0, i, 0)),
                   pl.BlockSpec((N_KV_HEADS, HEAD_DIM, TM), lambda i: (0, 0, i))],
        out_shape=[jax.ShapeDtypeStruct((N_TOK, W_ATTN), bf16), jax.ShapeDtypeStruct((N_TOK, W_KV), f32),
                   jax.ShapeDtypeStruct((N_KV_HEADS, N_TOK, HEAD_DIM), bf16),
                   jax.ShapeDtypeStruct((N_KV_HEADS, HEAD_DIM, N_TOK), bf16)],
        compiler_params=_params(1),
        name="qk_prep",
    )(q, kv, jnp.tile(q_norm, rep).reshape(1, W_KV), jnp.tile(k_norm, rep).reshape(1, W_KV),
      cos_t, sin_t, bd)


_NT_DIMS = (((1,), (1,)), ((), ()))
_ATT_CK = 256


def _attn_kernel(hps, has_cache, seq_len, q_ref, k_ref, vt_ref, *rest):
    if has_cache:
        kc_ref, vct_ref, o_ref, s_ref = rest
    else:
        o_ref, s_ref = rest
    chunks = ([None] if has_cache else []) + list(range(seq_len // _ATT_CK))
    tq = q_ref.shape[0]
    groups = _ATT_CK // SUBLANES

    def k_chunk(g, c):
        return kc_ref[g] if c is None else k_ref[g, c * _ATT_CK:(c + 1) * _ATT_CK, :]

    def vt_chunk(g, c):
        return vct_ref[g] if c is None else vt_ref[g, :, c * _ATT_CK:(c + 1) * _ATT_CK]

    def rows(i):
        return slice(i * _ATT_CK, (i + 1) * _ATT_CK)

    def score_chunk(h, i, m8):
        qh = q_ref[:, h * HEAD_DIM:(h + 1) * HEAD_DIM]
        s = lax.dot_general(k_chunk(h // Q_PER_KV, chunks[i]), qh, _NT_DIMS, preferred_element_type=f32)
        s_ref[h % 2, rows(i), :] = s
        return jnp.maximum(m8, jnp.max(s.reshape(groups, SUBLANES, tq), axis=0))

    neg = jnp.full((SUBLANES, tq), -jnp.inf, f32)
    m8 = neg
    for i in range(len(chunks)):
        m8 = score_chunk(0, i, m8)
    for h in range(hps):
        m = jnp.max(m8, axis=0, keepdims=True)
        m8 = neg
        l8 = jnp.zeros((SUBLANES, tq), f32)
        acc = jnp.zeros((HEAD_DIM, tq), f32)
        for i in range(len(chunks)):
            if h + 1 < hps:
                m8 = score_chunk(h + 1, i, m8)
            p = jnp.exp2(s_ref[h % 2, rows(i), :] - m)
            l8 = l8 + jnp.sum(p.reshape(groups, SUBLANES, tq), axis=0)
            acc = acc + _mm(vt_chunk(h // Q_PER_KV, chunks[i]), p.astype(bf16))
        l = jnp.sum(l8, axis=0, keepdims=True)
        o_ref[:, h * HEAD_DIM:(h + 1) * HEAD_DIM] = (acc / l).T.astype(bf16)


def _attn_call(qr, khm, vthm, n_seq, seq_len, hps, tq, tok0, cache=None):
    kvb = hps // Q_PER_KV
    qtiles = seq_len // tq
    row0 = tok0 // tq
    seq0 = tok0 // seq_len
    in_specs = [
        pl.BlockSpec((tq, hps * HEAD_DIM), lambda b, g, t: (row0 + b * qtiles + t, g)),
        pl.BlockSpec((kvb, seq_len, HEAD_DIM), lambda b, g, t: (g, seq0 + b, 0)),
        pl.BlockSpec((kvb, HEAD_DIM, seq_len), lambda b, g, t: (g, 0, seq0 + b)),
    ]
    args = [qr, khm, vthm]
    if cache is not None:
        in_specs += [pl.BlockSpec((None, kvb, PAST_LEN, HEAD_DIM), lambda b, g, t: (b, g, 0, 0)),
                     pl.BlockSpec((None, kvb, HEAD_DIM, PAST_LEN), lambda b, g, t: (b, g, 0, 0))]
        args += list(cache)
    return pl.pallas_call(
        functools.partial(_attn_kernel, hps, cache is not None, seq_len),
        grid=(n_seq, N_Q_HEADS // hps, qtiles),
        in_specs=in_specs,
        out_specs=pl.BlockSpec((tq, hps * HEAD_DIM), lambda b, g, t: (b * qtiles + t, g)),
        out_shape=jax.ShapeDtypeStruct((n_seq * seq_len, W_ATTN), bf16),
        scratch_shapes=[pltpu.VMEM((2, seq_len + (PAST_LEN if cache is not None else 0), tq), f32)],
        compiler_params=_params(3, VMEM_LIMIT),
        name="attention",
    )(*args)


def _merge_kernel(x_ref, mod_ref, pa_ref, pbc_ref, pbl_ref, pc_ref, attc_ref, attl_ref,
                  sa_ref, sb_ref, sc_ref, sd_ref, gm_ref,
                  woa_ref, wob_ref, woc_ref, wod_ref, wout_ref, o_ref):
    is_ctx = pl.program_id(0) < CTX_TILES

    def gated(p, s_ref):
        return (p.astype(f32) * s_ref[...].astype(f32)).astype(bf16)

    def gm(j):
        return gm_ref[:, j * D_MODEL:(j + 1) * D_MODEL].astype(f32)

    pb = jnp.where(is_ctx, pbc_ref[...], pbl_ref[...])
    att = jnp.where(is_ctx, attc_ref[...], attl_ref[...])
    merged = gm(0) * _mm(gated(pa_ref[...], sa_ref), woa_ref[...])
    merged = merged + gm(1) * _mm(gated(pb, sb_ref), wob_ref[...])
    merged = merged + gm(2) * _mm(gated(pc_ref[...], sc_ref), woc_ref[...])
    merged = merged + gm(3) * _mm(gated(att, sd_ref), wod_ref[...])
    y = _mm(merged.astype(bf16), wout_ref[...])
    o_ref[...] = x_ref[...] + mod_ref[2:3, :] * y


def _merge_call(x, mod_tiles_l, pa, pb_ctx, pb_lat, pc, att_ctx, att_lat, sa, sb, sc, sd, gm,
                woa, wob, woc, wod, wout):
    tile = lambda w: pl.BlockSpec((TM, w), lambda i: (i, 0))
    ctx_tile = lambda w: pl.BlockSpec((TM, w), lambda i: (jnp.minimum(i, CTX_TILES - 1), 0))
    lat_tile = lambda w: pl.BlockSpec((TM, w), lambda i: (jnp.maximum(i - CTX_TILES, 0), 0))
    const = lambda shape: pl.BlockSpec(shape, lambda i: (0, 0))
    return pl.pallas_call(
        _merge_kernel,
        grid=(N_TILES,),
        in_specs=[tile(D_MODEL), pl.BlockSpec((None, 3, D_MODEL), lambda i: (i, 0, 0)),
                  tile(W_BR), ctx_tile(W_BR), lat_tile(W_BR), tile(W_BR),
                  ctx_tile(W_ATTN), lat_tile(W_ATTN),
                  tile(W_BR), tile(W_BR), tile(W_BR), tile(W_ATTN), tile(4 * D_MODEL),
                  const((W_BR, D_MODEL)), const((W_BR, D_MODEL)), const((W_BR, D_MODEL)),
                  const((W_ATTN, D_MODEL)), const((D_MODEL, D_MODEL))],
        out_specs=tile(D_MODEL),
        out_shape=jax.ShapeDtypeStruct((N_TOK, D_MODEL), f32),
        compiler_params=_params(1, VMEM_LIMIT),
        name="merge_out",
    )(x, mod_tiles_l, pa, pb_ctx, pb_lat, pc, att_ctx, att_lat, sa, sb, sc, sd, gm,
      woa, wob, woc, wod, wout)


def kernel(x_prompt, x_sample, cache_k, cache_v, c, c_ctx, w_ada, b_ada, norm_g, w_in, conv_dw_w, conv_dw_b, conv_ln_g, conv_ln_b, conv_pw, hy_short_w, hy_short_b, hy_w1, hy_b1, hy_freq, hy_w2, hy_b2, hy_w3, hy_b3, hy_skip, pool_w, pool_scale, q_norm, k_norm, wo_conv, wo_hyena, wo_pool, wo_attn, w_out):
    x = jnp.concatenate([x_prompt.reshape(N_CTX_TOK, D_MODEL), x_sample.reshape(N_LAT_TOK, D_MODEL)], axis=0)

    cond8 = jnp.concatenate([c_ctx[None, :], c, jnp.zeros((8 - 1 - DEC_BATCH, D_MODEL), f32)], axis=0)
    mod = _ada_call(cond8, w_ada, b_ada)
    tile_cond = np.concatenate([np.zeros(CTX_TILES, np.int32),
                                1 + np.arange(N_TILES - CTX_TILES, dtype=np.int32) // LAT_TPS])
    mod_tiles = mod[:, tile_cond].reshape(DEPTH, N_TILES, 3, D_MODEL)

    seq_lens = (SEQ, DEC_SEQ)
    tabs = {L: _dft_tables(2 * L // DFT_B) for L in seq_lens}
    feats = {L: _filter_features(L) for L in seq_lens}
    absd = _abs_deltas()
    cos_t, sin_t = _rope_tables()
    bd = _group_sum_matrix()
    w1p = jnp.pad(hy_w1, ((0, 0), (0, _FEAT_PAD - FILTER_EMB), (0, 0)))

    ks, vs = [], []
    for l in range(DEPTH):
        (a, sa, bp, sb, ci, sc, q, kv, sd, gm) = _inproj_call(x, mod_tiles[l], norm_g[l], w_in[l].astype(bf16))

        pa = _conformer_call(a, conv_dw_w[l], conv_dw_b[l], conv_ln_g[l], conv_ln_b[l], conv_pw[l].astype(bf16))
        pc = _pool_call(ci, pool_w[l].astype(bf16), pool_scale[l])

        v, x1, x2 = _shortconv_call(bp, hy_short_w[l], hy_short_b[l])
        spec = {}
        for L in seq_lens:
            hfilt, ss = _filter_mlp_call(L, feats[L][0], feats[L][1], w1p[l], hy_b1[l], hy_freq[l], hy_w2[l],
                                         hy_b2[l], hy_w3[l], hy_b3[l], absd)
            spec[L] = _filter_spectrum(2 * L // DFT_B, tabs[L][0], hfilt, ss)
        pb_ctx = _hy_ctx_call(tabs[SEQ], spec[SEQ], v, x1, x2, hy_skip[l])
        pb_lat = _hy_lat_call(tabs[DEC_SEQ], spec[DEC_SEQ], v, x1, x2, hy_skip[l]).reshape(N_LAT_TOK, W_BR)

        qr, kn, khm, vthm = _qk_prep_call(q, kv, q_norm[l], k_norm[l], cos_t, sin_t, bd)
        att_ctx = _attn_call(qr, khm, vthm, BATCH, SEQ, N_Q_HEADS, SEQ, 0)
        cache = (cache_k[:, l].astype(bf16).transpose(0, 2, 1, 3), cache_v[:, l].astype(bf16).transpose(0, 2, 3, 1))
        att_lat = _attn_call(qr, khm, vthm, DEC_BATCH, DEC_SEQ, Q_PER_KV, 256, N_CTX_TOK, cache)

        x = _merge_call(x, mod_tiles[l], pa, pb_ctx, pb_lat, pc, att_ctx, att_lat, sa, sb, sc, sd, gm,
                        wo_conv[l].astype(bf16), wo_hyena[l].astype(bf16), wo_pool[l].astype(bf16),
                        wo_attn[l].astype(bf16), w_out[l].astype(bf16))

        ks.append(kn[:N_CTX_TOK].reshape(BATCH, SEQ, N_KV_HEADS, HEAD_DIM))
        vs.append(kv[:N_CTX_TOK, W_KV:].reshape(BATCH, SEQ, N_KV_HEADS, HEAD_DIM))

    y_prompt = x[:N_CTX_TOK].reshape(BATCH, SEQ, D_MODEL)
    y_sample = x[N_CTX_TOK:].reshape(DEC_BATCH, DEC_SEQ, D_MODEL)
    return (y_prompt, y_sample, jnp.stack(ks, axis=1), jnp.stack(vs, axis=1))
```

```python
import functools
import math

import numpy as np
import jax
import jax.numpy as jnp
from jax import lax
from jax.experimental import pallas as pl
from jax.experimental.pallas import tpu as pltpu

f32 = jnp.float32
bf16 = jnp.bfloat16

D_MODEL = 1024
BATCH = 16
SEQ = 256
DEPTH = 2
DEC_BATCH = 2
DEC_SEQ = 4096
PAST_LEN = 256
GRID_W = 64
W_BR = 512
CONV_K = 31
SHORT_K = 3
FILTER_BANDS = 16
FILTER_EMB = 1 + 2 * FILTER_BANDS
FILTER_HIDDEN = 64
DECAY_TARGET = 1e-2
FAST_DECAY_PCT = 0.3
SLOW_DECAY_PCT = 1.5
POOL_WINDOWS = (2, 4, 8, 16)
POOL_GROUP = W_BR // len(POOL_WINDOWS)
HEAD_DIM = 64
N_Q_HEADS = 16
N_KV_HEADS = 4
Q_PER_KV = N_Q_HEADS // N_KV_HEADS
W_ATTN = N_Q_HEADS * HEAD_DIM
W_KV = N_KV_HEADS * HEAD_DIM
ROPE_THETA = 10000.0
EPS = 1e-6
N_IN = 2 * W_BR + W_BR + 3 * W_BR + W_BR + W_BR + W_BR + W_ATTN + 2 * W_KV + W_ATTN + 4 * D_MODEL

TM = 256
N_CTX_TOK = BATCH * SEQ
N_LAT_TOK = DEC_BATCH * DEC_SEQ
N_TOK = N_CTX_TOK + N_LAT_TOK
N_TILES = N_TOK // TM
CTX_TILES = N_CTX_TOK // TM
LAT_TPS = DEC_SEQ // TM
DFT_B = 256
LANES = 128
SUBLANES = 8
VMEM_LIMIT = 56 * 1024 * 1024

assert SEQ == TM == DFT_B and CTX_TILES % LAT_TPS == 0


def _sigmoid(x):
    return 1.0 / (1.0 + jnp.exp(-x))


def _silu(x):
    return x * _sigmoid(x)


def _mm(a, b):
    return jnp.dot(a, b, preferred_element_type=f32)


def _split_bf16(a):
    hi = a.astype(bf16)
    lo = (a - hi.astype(f32)).astype(bf16)
    return hi, lo


def _mm3(a, b):
    ah, al = _split_bf16(a)
    bh, bl = _split_bf16(b)
    return _mm(ah, bh) + (_mm(ah, bl) + _mm(al, bh))


def _params(n_axes, vmem=None):
    return pltpu.CompilerParams(dimension_semantics=("arbitrary",) * n_axes,
                                vmem_limit_bytes=vmem)


def _tile_position():
    i = pl.program_id(0)
    return i >= CTX_TILES, lax.rem(i, LAT_TPS)


def _halo_flags():
    is_lat, jt = _tile_position()
    return jnp.logical_and(is_lat, jt > 0), jnp.logical_and(is_lat, jt < LAT_TPS - 1)


def _halo_specs(width, hb):
    r = TM // hb
    last = N_TOK // hb - 1
    return [
        pl.BlockSpec((hb, width), lambda i: (jnp.maximum(i * r - 1, 0), 0)),
        pl.BlockSpec((TM, width), lambda i: (i, 0)),
        pl.BlockSpec((hb, width), lambda i: (jnp.minimum((i + 1) * r, last), 0)),
    ]


def _ada_kernel(c_ref, w_ref, b_ref, o_ref):
    c = c_ref[...]
    o_ref[...] = _mm(_silu(c).astype(bf16), w_ref[...].astype(bf16)) + b_ref[...]


def _ada_call(cond8, w_ada, b_ada):
    nb = 3 * D_MODEL // D_MODEL
    return pl.pallas_call(
        _ada_kernel,
        grid=(DEPTH, nb),
        in_specs=[
            pl.BlockSpec((8, D_MODEL), lambda l, j: (0, 0)),
            pl.BlockSpec((None, D_MODEL, D_MODEL), lambda l, j: (l, 0, j)),
            pl.BlockSpec((None, 1, D_MODEL), lambda l, j: (l, 0, j)),
        ],
        out_specs=pl.BlockSpec((None, 8, D_MODEL), lambda l, j: (l, 0, j)),
        out_shape=jax.ShapeDtypeStruct((DEPTH, 8, 3 * D_MODEL), f32),
        compiler_params=_params(2, VMEM_LIMIT),
        name="ada_mod",
    )(cond8, w_ada, b_ada.reshape(DEPTH, 1, 3 * D_MODEL))


_C_GLU = 0
_C_AGATE = 2 * W_BR
_C_BPROJ = _C_AGATE + W_BR
_C_BGATE = _C_BPROJ + 3 * W_BR
_C_CIN = _C_BGATE + W_BR
_C_CGATE = _C_CIN + W_BR
_C_Q = _C_CGATE + W_BR
_C_KV = _C_Q + W_ATTN
_C_DGATE = _C_KV + 2 * W_KV
_C_GM = _C_DGATE + W_ATTN
assert _C_GM + 4 * D_MODEL == N_IN


_CONF_HB = 16
_CONF_CHUNK = 32
_CONF_SPAN = TM + 2 * _CONF_HB - SUBLANES
assert (CONV_K - 1 + _CONF_HB - CONV_K // 2) // SUBLANES * SUBLANES + TM <= _CONF_SPAN


def _conformer_tasks(pad_ref, shift_ref, conv_ref, dw_ref, dwb_ref, lng_ref, lnb_ref, pw_ref, o_ref):
    def shifted_copies():
        for r in range(SUBLANES):
            shift_ref[r] = pad_ref[r:r + _CONF_SPAN, :]

    def conv_chunk(c):
        r0 = c * _CONF_CHUNK
        off = _CONF_HB - CONV_K // 2
        groups = _CONF_CHUNK // SUBLANES
        acc = jnp.broadcast_to(dwb_ref[...][None], (groups, SUBLANES, W_BR))
        for j in range(CONV_K):
            a0 = r0 + (j + off) // SUBLANES * SUBLANES
            x = shift_ref[(j + off) % SUBLANES, a0:a0 + _CONF_CHUNK, :]
            acc = acc + dw_ref[j][None] * x.reshape(groups, SUBLANES, W_BR)
        conv_ref[r0:r0 + _CONF_CHUNK, :] = acc.reshape(_CONF_CHUNK, W_BR)

    def finish():
        a = conv_ref[...]
        mu = jnp.mean(a, axis=-1, keepdims=True)
        xc = a - mu
        var = jnp.mean(xc * xc, axis=-1, keepdims=True)
        y = xc * lax.rsqrt(var + EPS) * lng_ref[...] + lnb_ref[...]
        o_ref[...] = _mm(_silu(y).astype(bf16), pw_ref[...]).astype(bf16)

    return ([shifted_copies] + [functools.partial(conv_chunk, c) for c in range(TM // _CONF_CHUNK)]
            + [finish])


def _short_conv_tasks(pad_ref, w_ref, b_ref, outs):
    def group(g):
        cs = slice(g * W_BR, (g + 1) * W_BR)
        acc = jnp.broadcast_to(b_ref[:, cs], (TM, W_BR))
        for j in range(SHORT_K):
            r0 = _CONF_HB + j - SHORT_K // 2
            acc = acc + w_ref[j:j + 1, cs] * pad_ref[r0:r0 + TM, cs]
        outs[g][...] = acc

    return [functools.partial(group, g) for g in range(len(outs))]


def _pool_tasks(pad_ref, pw_ref, ps_ref, o_ref):
    def group(g):
        is_lat, jt = _tile_position()
        seq_len = jnp.where(is_lat, DEC_SEQ, SEQ)
        t = jnp.where(is_lat, jt * TM, 0) + lax.broadcasted_iota(jnp.int32, (TM, POOL_GROUP), 0)
        cs = slice(g * POOL_GROUP, (g + 1) * POOL_GROUP)
        hw = POOL_WINDOWS[g] // 2
        s = pad_ref[_CONF_HB - hw:_CONF_HB - hw + TM, cs]
        for d in range(-hw + 1, hw):
            s = s + pad_ref[_CONF_HB + d:_CONF_HB + d + TM, cs]
        cnt = (jnp.minimum(t + hw, seq_len) - jnp.maximum(t - hw, 0)).astype(f32)
        pooled = s / cnt - pad_ref[_CONF_HB:_CONF_HB + TM, cs]
        y = _mm(pooled.astype(bf16), pw_ref[g])
        o_ref[:, cs] = (y * ps_ref[:, cs]).astype(bf16)

    return [functools.partial(group, g) for g in range(len(POOL_WINDOWS))]


def _inproj_kernel(xp_ref, x_ref, xn_ref, mod_ref, g_ref, w_ref,
                   dw_ref, dwb_ref, lng_ref, lnb_ref, cpw_ref, sw_ref, sbias_ref, ppw_ref, pps_ref,
                   pa_ref, sa_ref, v_ref, x1_ref, x2_ref, sb_ref, pc_ref, sc_ref, q_ref, kv_ref, sd_ref, gm_ref,
                   pad_a, shift_a, conv_a, pad_b, pad_c):
    has_prev, has_next = _halo_flags()
    gain = g_ref[...]
    scale1 = 1.0 + mod_ref[1:2, :]
    shift = mod_ref[0:1, :]

    def modulated(x):
        ms = jnp.mean(x * x, axis=-1, keepdims=True)
        return (x * lax.rsqrt(ms + EPS) * gain * scale1 + shift).astype(bf16)

    h = modulated(x_ref[...])
    h_ext = jnp.concatenate([modulated(xp_ref[...]), h, modulated(xn_ref[...])], axis=0)
    rows = TM + 2 * _CONF_HB
    r = lax.broadcasted_iota(jnp.int32, (rows, 1), 0)
    valid = jnp.logical_and(jnp.logical_or(r >= _CONF_HB, has_prev),
                            jnp.logical_or(r < _CONF_HB + TM, has_next))

    def proj(c0, width):
        return _mm(h, w_ref[:, c0:c0 + width])

    def proj_ext(c0, width):
        return _mm(h_ext, w_ref[:, c0:c0 + width])

    def ext_into(dst, c0, col):
        def task():
            dst[:, col * W_BR:(col + 1) * W_BR] = jnp.where(valid, proj_ext(c0, W_BR), 0.0)
        return task

    def act_into(dst, c0, col, act, width=W_BR):
        def task():
            y = proj(c0, width)
            dst[:, col * W_BR:col * W_BR + width] = (y if act is None else act(y)).astype(dst.dtype)
        return task

    pad_a[...] = jnp.where(valid, proj_ext(_C_GLU, W_BR) * _sigmoid(proj_ext(_C_GLU + W_BR, W_BR)), 0.0)

    mxu_tasks = [ext_into(pad_b, _C_BPROJ + j * W_BR, j) for j in range(3)] + [ext_into(pad_c, _C_CIN, 0)]
    mxu_tasks += [act_into(sa_ref, _C_AGATE, 0, _silu), act_into(sb_ref, _C_BGATE, 0, _silu),
                  act_into(sc_ref, _C_CGATE, 0, _silu)]
    mxu_tasks += [act_into(q_ref, _C_Q + j * W_BR, j, None) for j in range(W_ATTN // W_BR)]
    mxu_tasks += [act_into(kv_ref, _C_KV, 0, None, 2 * W_KV)]
    mxu_tasks += [act_into(sd_ref, _C_DGATE + j * W_BR, j, _silu) for j in range(W_ATTN // W_BR)]
    mxu_tasks += [act_into(gm_ref, _C_GM + j * W_BR, j, _sigmoid) for j in range(4 * D_MODEL // W_BR)]

    vpu_tasks = _conformer_tasks(pad_a, shift_a, conv_a, dw_ref, dwb_ref, lng_ref, lnb_ref, cpw_ref, pa_ref)
    n_ext = 4
    later = (_short_conv_tasks(pad_b, sw_ref, sbias_ref, (v_ref, x1_ref, x2_ref))
             + _pool_tasks(pad_c, ppw_ref, pps_ref, pc_ref))
    vpu_tasks = vpu_tasks[:n_ext + 1] + later + vpu_tasks[n_ext + 1:]
    for k in range(max(len(mxu_tasks), len(vpu_tasks))):
        if k < len(mxu_tasks):
            mxu_tasks[k]()
        if k < len(vpu_tasks):
            vpu_tasks[k]()


def _inproj_call(x, mod_tiles_l, norm_g_l, w_in_l, dw_w, dw_b, ln_g, ln_b, conv_pw_bf, short_w, short_b,
                 pool_w_bf, pool_scale):
    widths = [(W_BR, bf16), (W_BR, bf16), (W_BR, f32), (W_BR, f32), (W_BR, f32), (W_BR, bf16),
              (W_BR, bf16), (W_BR, bf16), (W_ATTN, f32), (2 * W_KV, f32), (W_ATTN, bf16), (4 * D_MODEL, bf16)]
    row = lambda v: v.reshape(1, -1)
    rep = lambda v: jnp.broadcast_to(v[..., None, :], v.shape[:-1] + (SUBLANES, W_BR))
    const = lambda shape: pl.BlockSpec(shape, lambda i: (0,) * len(shape))
    ext_rows = TM + 2 * _CONF_HB
    return pl.pallas_call(
        _inproj_kernel,
        grid=(N_TILES,),
        in_specs=_halo_specs(D_MODEL, _CONF_HB) + [
            pl.BlockSpec((None, 3, D_MODEL), lambda i: (i, 0, 0)),
            const((1, D_MODEL)),
            pl.BlockSpec((D_MODEL, N_IN), lambda i: (0, 0), pipeline_mode=pl.Buffered(1)),
            const((CONV_K, SUBLANES, W_BR)), const((SUBLANES, W_BR)), const((1, W_BR)), const((1, W_BR)),
            const((W_BR, W_BR)), const((SHORT_K, 3 * W_BR)), const((1, 3 * W_BR)),
            const((len(POOL_WINDOWS), POOL_GROUP, POOL_GROUP)), const((1, W_BR))],
        out_specs=[pl.BlockSpec((TM, w), lambda i: (i, 0)) for w, _ in widths],
        out_shape=[jax.ShapeDtypeStruct((N_TOK, w), dt) for w, dt in widths],
        scratch_shapes=[pltpu.VMEM((ext_rows, W_BR), f32), pltpu.VMEM((SUBLANES, _CONF_SPAN, W_BR), f32),
                        pltpu.VMEM((TM, W_BR), f32), pltpu.VMEM((ext_rows, 3 * W_BR), f32),
                        pltpu.VMEM((ext_rows, W_BR), f32)],
        compiler_params=_params(1, VMEM_LIMIT),
        name="in_proj",
    )(x, x, x, mod_tiles_l, row(norm_g_l), w_in_l, rep(dw_w), rep(dw_b), row(ln_g), row(ln_b), conv_pw_bf,
      short_w, row(short_b), pool_w_bf, row(pool_scale))


def _dft_tables(n2):
    n = DFT_B * n2
    k1 = np.arange(DFT_B, dtype=np.int64)[:, None]
    t1 = np.arange(DFT_B, dtype=np.int64)[None, :]
    fwd, inv = [], []
    for k2 in range(n2):
        ang = -2.0 * np.pi * (((n2 * k1 + k2) * t1) % n).astype(np.float64) / n
        gr, gi = np.cos(ang), np.sin(ang)
        fwd.append(np.concatenate([gr, gi], axis=0))
        inv.append(np.concatenate([gr.T, gi.T], axis=1))
    return (jnp.asarray(np.stack(fwd).astype(np.float32)).astype(bf16),
            jnp.asarray(np.stack(inv).astype(np.float32)).astype(bf16))


def _cmul_const(x, ang):
    c, s = math.cos(ang), math.sin(ang)
    xr, xi = x
    return (c * xr - s * xi, s * xr + c * xi)


def _fft_list(xs, sign):
    n = len(xs)
    if n == 1:
        return list(xs)
    ev = _fft_list(xs[0::2], sign)
    od = _fft_list(xs[1::2], sign)
    out = [None] * n
    for k in range(n // 2):
        orr, oi = od[k]
        if k == 0:
            tr, ti = orr, oi
        elif 4 * k == n:
            tr, ti = (-oi, orr) if sign > 0 else (oi, -orr)
        else:
            tr, ti = _cmul_const(od[k], sign * 2.0 * math.pi * k / n)
        er, ei = ev[k]
        out[k] = (er + tr, ei + ti)
        out[k + n // 2] = (er - tr, ei - ti)
    return out


def _fft_padded(xs, sign):
    m = len(xs)
    ev = _fft_list(xs, sign)
    od = _fft_list([xs[0]] + [_cmul_const(xs[t], sign * math.pi * t / m) for t in range(1, m)], sign)
    out = []
    for j in range(m):
        out += [ev[j], od[j]]
    return out


def _ifft_truncated(ws):
    m = len(ws) // 2
    ev = _fft_list(ws[0::2], 1)
    od = _fft_list(ws[1::2], 1)
    out = []
    for t in range(m):
        tr, ti = od[t] if t == 0 else _cmul_const(od[t], math.pi * t / m)
        out.append((ev[t][0] + tr, ev[t][1] + ti))
    return out


def _spectral_block(u, m, mt, hr, hi):
    w = hr.shape[1]
    p = _mm(m, u)
    xr = p[:DFT_B, :w] - p[DFT_B:, w:]
    xi = p[:DFT_B, w:] + p[DFT_B:, :w]
    yr = xr * hr - xi * hi
    yi = xr * hi + xi * hr
    rhs = jnp.concatenate([jnp.concatenate([yr, yi], axis=1),
                           jnp.concatenate([yi, -yr], axis=1)], axis=0).astype(bf16)
    return _mm(mt, rhs)


def _slab(s):
    return pl.ds(pl.multiple_of(s * SUBLANES, SUBLANES), SUBLANES)


_FEAT_PAD = 64


def _filter_features(seq_len):
    m = np.arange(2 * seq_len)
    j = np.where(m < seq_len, m, 2 * seq_len - m).astype(np.float64)
    t = j / (seq_len - 1)
    bands = np.linspace(1e-4, FILTER_BANDS - 1, FILTER_BANDS)[None, :]
    w = (2.0 * math.pi / seq_len) * j[:, None]
    z = np.concatenate([t[:, None], np.cos(bands * w), np.sin(bands * w)], axis=-1)
    z = np.pad(z, ((0, 0), (0, _FEAT_PAD - FILTER_EMB)))
    return jnp.asarray(z.astype(np.float32)), jnp.asarray(t.astype(np.float32)[:, None])


def _abs_deltas():
    max_decay = math.log(DECAY_TARGET) / FAST_DECAY_PCT
    min_decay = math.log(DECAY_TARGET) / SLOW_DECAY_PCT
    d = np.abs(np.linspace(min_decay, max_decay, W_BR))
    return jnp.asarray(np.concatenate([d, d]).astype(np.float32)[None, :])


def _filter_mlp_kernel(seq_len, tr, z_ref, t_ref, w1_ref, b1_ref, fr_ref, w2_ref, b2_ref, w3_ref, b3_ref,
                       ad_ref, h_ref, ss_ref):
    i = pl.program_id(0)
    fr = fr_ref[...]
    hdn = jnp.sin(fr * (_mm3(z_ref[...], w1_ref[...]) + b1_ref[...]))
    hdn = jnp.sin(fr * (_mm3(hdn, w2_ref[...]) + b2_ref[...]))
    h = _mm3(hdn, w3_ref[...]) + b3_ref[...]
    h = h * jnp.exp(-t_ref[...] * ad_ref[...])
    m = i * tr + lax.broadcasted_iota(jnp.int32, h.shape, 0)
    h = jnp.where(m == seq_len, 0.0, h)
    h_ref[...] = h

    @pl.when(i == 0)
    def _():
        ss_ref[...] = jnp.zeros_like(ss_ref)

    ss_ref[...] += jnp.sum(h * h, axis=0, keepdims=True)


def _filter_mlp_call(seq_len, feats, tcol, w1p, b1, freq, w2, b2, w3, b3, absd):
    tr = min(512, seq_len)
    steps = 2 * seq_len // tr
    half_steps = seq_len // tr
    wide = 2 * W_BR
    const = lambda shape: pl.BlockSpec(shape, lambda i: (0, 0))
    row = lambda v: v.reshape(1, -1)
    return pl.pallas_call(
        functools.partial(_filter_mlp_kernel, seq_len, tr),
        grid=(steps,),
        in_specs=[
            pl.BlockSpec((tr, _FEAT_PAD), lambda i: (i, 0)),
            pl.BlockSpec((tr, 1), lambda i: (i, 0)),
            const((_FEAT_PAD, FILTER_HIDDEN)), const((1, FILTER_HIDDEN)), const((1, FILTER_HIDDEN)),
            const((FILTER_HIDDEN, FILTER_HIDDEN)), const((1, FILTER_HIDDEN)),
            pl.BlockSpec((FILTER_HIDDEN, wide), lambda i: (0, i // half_steps)),
            pl.BlockSpec((1, wide), lambda i: (0, i // half_steps)),
            const((1, wide)),
        ],
        out_specs=[pl.BlockSpec((tr, wide), lambda i: (i, 0)), const((1, wide))],
        out_shape=[jax.ShapeDtypeStruct((2 * seq_len, wide), f32), jax.ShapeDtypeStruct((1, wide), f32)],
        compiler_params=_params(1),
        name="filter_mlp",
    )(feats, tcol, w1p, row(b1), row(freq), w2, row(b2), w3, row(b3), absd)


_RADIX_ROWS = 128


def _filter_radix_kernel(n2, h_ref, ur_ref, ui_ref, sr_ref, si_ref):
    zero = jnp.zeros((SUBLANES, LANES), f32)

    def body(s, carry):
        rs = _slab(s)
        us = _fft_list([(h_ref[t, rs, :], zero) for t in range(n2)], -1)
        for k in range(n2):
            sr_ref[k, rs, :] = us[k][0]
            si_ref[k, rs, :] = us[k][1]
        return carry

    lax.fori_loop(0, _RADIX_ROWS // SUBLANES, body, 0)
    for k in range(n2):
        ur_ref[k] = sr_ref[k].astype(bf16)
        ui_ref[k] = si_ref[k].astype(bf16)


def _filter_spec_kernel(scale_const, ur_ref, ui_ref, m_ref, ss_ref, hr_ref, hi_ref):
    w = ur_ref.shape[1]
    p = _mm(m_ref[...], jnp.concatenate([ur_ref[...], ui_ref[...]], axis=1))
    scale = lax.rsqrt(ss_ref[...] + EPS) * scale_const
    hr_ref[...] = (p[:DFT_B, :w] - p[DFT_B:, w:]) * scale
    hi_ref[...] = (p[:DFT_B, w:] + p[DFT_B:, :w]) * scale


def _filter_spectrum(n2, mfwd, hfilt, ss):
    wide = 2 * W_BR
    blk = pl.BlockSpec((n2, _RADIX_ROWS, LANES), lambda r, c: (0, r, c))
    u_shape = jax.ShapeDtypeStruct((n2, DFT_B, wide), bf16)
    ur, ui = pl.pallas_call(
        functools.partial(_filter_radix_kernel, n2),
        grid=(DFT_B // _RADIX_ROWS, wide // LANES),
        in_specs=[blk],
        out_specs=[blk, blk],
        out_shape=[u_shape, u_shape],
        scratch_shapes=[pltpu.VMEM((n2, _RADIX_ROWS, LANES), f32)] * 2,
        compiler_params=_params(2),
        name="filter_radix",
    )(hfilt.reshape(n2, DFT_B, wide))
    kspec = pl.BlockSpec((None, DFT_B, wide), lambda k: (k, 0, 0))
    h_shape = jax.ShapeDtypeStruct((n2, DFT_B, wide), f32)
    return pl.pallas_call(
        functools.partial(_filter_spec_kernel, 1.0 / (n2 * DFT_B)),
        grid=(n2,),
        in_specs=[kspec, kspec, pl.BlockSpec((None, 2 * DFT_B, DFT_B), lambda k: (k, 0, 0)),
                  pl.BlockSpec((1, wide), lambda k: (0, 0))],
        out_specs=[kspec, kspec],
        out_shape=[h_shape, h_shape],
        compiler_params=_params(1),
        name="filter_spec",
    )(ur, ui, mfwd, ss)


def _hy_ctx_kernel(va_ref, vb_ref, x1a_ref, x1b_ref, x2a_ref, x2b_ref, m_ref, mt_ref, hr_ref, hi_ref,
                   skip_ref, o_ref):
    n2 = m_ref.shape[0]
    za, zb = va_ref[...], vb_ref[...]
    gates = ((x1a_ref, x1b_ref), (x2a_ref, x2b_ref))
    for order in range(2):
        cs = slice(order * W_BR, (order + 1) * W_BR)
        u = jnp.concatenate([za, zb], axis=1).astype(bf16)
        w = _spectral_block(u, m_ref[0], mt_ref[0], hr_ref[0, :, cs], hi_ref[0, :, cs])
        for k2 in range(1, n2):
            w = w + _spectral_block(u, m_ref[k2], mt_ref[k2], hr_ref[k2, :, cs], hi_ref[k2, :, cs])
        skip = skip_ref[order:order + 1, :]
        za = gates[order][0][...] * (w[:, :W_BR] + za * skip)
        zb = gates[order][1][...] * (w[:, W_BR:] + zb * skip)
    o_ref[0:DFT_B, :] = za
    o_ref[DFT_B:, :] = zb


def _hy_ctx_call(tabs, spec, v, x1, x2, skip):
    mfwd, minv = tabs
    n2 = mfwd.shape[0]
    assert n2 == 2
    seq = lambda par: pl.BlockSpec((DFT_B, W_BR), lambda p: (2 * p + par, 0))
    whole = lambda a: pl.BlockSpec(a.shape, lambda p: (0,) * a.ndim)
    return pl.pallas_call(
        _hy_ctx_kernel,
        grid=(BATCH // 2,),
        in_specs=[seq(0), seq(1), seq(0), seq(1), seq(0), seq(1), whole(mfwd), whole(minv),
                  whole(spec[0]), whole(spec[1]), whole(skip)],
        out_specs=pl.BlockSpec((2 * DFT_B, W_BR), lambda p: (p, 0)),
        out_shape=jax.ShapeDtypeStruct((N_CTX_TOK, W_BR), f32),
        compiler_params=_params(1, VMEM_LIMIT),
        name="hyena_ctx",
    )(v, v, x1, x1, x2, x2, mfwd, minv, spec[0], spec[1], skip)


def _store_radix(us, rs, sr_ref, si_ref):
    for k, (ur, ui) in enumerate(us):
        sr_ref[k, rs, :] = ur
        si_ref[k, rs, :] = ui


def _cast_radix(sr_ref, si_ref, ur_ref, ui_ref):
    for k in range(sr_ref.shape[0]):
        ur_ref[k] = sr_ref[k].astype(bf16)
        ui_ref[k] = si_ref[k].astype(bf16)


def _hy_fwd_radix_kernel(m, za_ref, zb_ref, ur_ref, ui_ref, sr_ref, si_ref):
    def body(s, carry):
        rs = _slab(s)
        _store_radix(_fft_padded([(za_ref[t, rs, :], zb_ref[t, rs, :]) for t in range(m)], -1),
                     rs, sr_ref, si_ref)
        return carry

    lax.fori_loop(0, _RADIX_ROWS // SUBLANES, body, 0)
    _cast_radix(sr_ref, si_ref, ur_ref, ui_ref)


def _hy_spec_kernel(ur_ref, ui_ref, m_ref, mt_ref, hr_ref, hi_ref, w_ref):
    u = jnp.concatenate([ur_ref[...], ui_ref[...]], axis=1)
    w_ref[...] = _spectral_block(u, m_ref[...], mt_ref[...], hr_ref[...], hi_ref[...])


def _hy_inv_radix_kernel(m, final, wr_ref, wi_ref, za_ref, zb_ref, ga_ref, gb_ref, skip_ref, o_ref, *rest):
    skip = skip_ref[...]

    def body(s, carry):
        rs = _slab(s)
        ys = _ifft_truncated([(wr_ref[k, rs, :], wi_ref[k, rs, :]) for k in range(2 * m)])
        zs = []
        for t in range(m):
            za = ga_ref[t, rs, :] * (ys[t][0] + za_ref[t, rs, :] * skip)
            zb = gb_ref[t, rs, :] * (ys[t][1] + zb_ref[t, rs, :] * skip)
            o_ref[0, t, rs, :] = za
            o_ref[1, t, rs, :] = zb
            zs.append((za, zb))
        if not final:
            _store_radix(_fft_padded(zs, -1), rs, rest[2], rest[3])
        return carry

    lax.fori_loop(0, _RADIX_ROWS // SUBLANES, body, 0)
    if not final:
        _cast_radix(rest[2], rest[3], rest[0], rest[1])


def _hy_lat_call(tabs, spec, v, x1, x2, skip):
    mfwd, minv = tabs
    n2 = mfwd.shape[0]
    m = n2 // 2
    assert m * DFT_B == DEC_SEQ and DEC_BATCH == 2
    seq0 = N_CTX_TOK // DEC_SEQ
    grid = (DFT_B // _RADIX_ROWS, W_BR // LANES)

    def tok(arr, b):
        return (arr.reshape(N_TOK // DEC_SEQ, m, DFT_B, W_BR),
                pl.BlockSpec((None, m, _RADIX_ROWS, LANES), lambda r, c: (seq0 + b, 0, r, c)))

    def pair(arr, b):
        return (arr, pl.BlockSpec((None, m, _RADIX_ROWS, LANES), lambda r, c: (b, 0, r, c)))

    u_spec = pl.BlockSpec((n2, _RADIX_ROWS, LANES), lambda r, c: (0, r, c))
    u_shape = jax.ShapeDtypeStruct((n2, DFT_B, W_BR), bf16)
    z_spec = pl.BlockSpec((2, m, _RADIX_ROWS, LANES), lambda r, c: (0, 0, r, c))
    z_shape = jax.ShapeDtypeStruct((2, m, DFT_B, W_BR), f32)
    scratch = [pltpu.VMEM((n2, _RADIX_ROWS, LANES), f32)] * 2

    def spectral(ur, ui, order):
        kspec = pl.BlockSpec((None, DFT_B, W_BR), lambda k: (k, 0, 0))
        hspec = pl.BlockSpec((None, DFT_B, W_BR), lambda k: (k, 0, order))
        return pl.pallas_call(
            _hy_spec_kernel,
            grid=(n2,),
            in_specs=[kspec, kspec, pl.BlockSpec((None, 2 * DFT_B, DFT_B), lambda k: (k, 0, 0)),
                      pl.BlockSpec((None, DFT_B, 2 * DFT_B), lambda k: (k, 0, 0)), hspec, hspec],
            out_specs=pl.BlockSpec((None, DFT_B, 2 * W_BR), lambda k: (k, 0, 0)),
            out_shape=jax.ShapeDtypeStruct((n2, DFT_B, 2 * W_BR), f32),
            compiler_params=_params(1),
            name="hyena_spec",
        )(ur, ui, mfwd, minv, spec[0], spec[1])

    def inverse(w, z_srcs, gate, order, final):
        wr_spec = pl.BlockSpec((n2, _RADIX_ROWS, LANES), lambda r, c: (0, r, c))
        wi_spec = pl.BlockSpec((n2, _RADIX_ROWS, LANES), lambda r, c: (0, r, W_BR // LANES + c))
        g_srcs = [tok(gate, 0), tok(gate, 1)]
        return pl.pallas_call(
            functools.partial(_hy_inv_radix_kernel, m, final),
            grid=grid,
            in_specs=[wr_spec, wi_spec, z_srcs[0][1], z_srcs[1][1], g_srcs[0][1], g_srcs[1][1],
                      pl.BlockSpec((1, LANES), lambda r, c: (0, c))],
            out_specs=z_spec if final else [z_spec, u_spec, u_spec],
            out_shape=z_shape if final else [z_shape, u_shape, u_shape],
            scratch_shapes=[] if final else scratch,
            compiler_params=_params(2, VMEM_LIMIT),
            name="hyena_inv_radix",
        )(w, w, z_srcs[0][0], z_srcs[1][0], g_srcs[0][0], g_srcs[1][0], skip[order].reshape(1, W_BR))

    v_srcs = [tok(v, 0), tok(v, 1)]
    ur, ui = pl.pallas_call(
        functools.partial(_hy_fwd_radix_kernel, m),
        grid=grid,
        in_specs=[v_srcs[0][1], v_srcs[1][1]],
        out_specs=[u_spec, u_spec],
        out_shape=[u_shape, u_shape],
        scratch_shapes=scratch,
        compiler_params=_params(2),
        name="hyena_fwd_radix",
    )(v_srcs[0][0], v_srcs[1][0])
    z1, ur, ui = inverse(spectral(ur, ui, 0), v_srcs, x1, 0, False)
    return inverse(spectral(ur, ui, 1), [pair(z1, 0), pair(z1, 1)], x2, 1, True)


_QK_SCALE = HEAD_DIM ** -0.5 * math.log2(math.e)


def _rope_tables():
    half = HEAD_DIM // 2
    f = half // 2
    inv = ROPE_THETA ** (-np.arange(f, dtype=np.float64) / f)
    pos = np.arange(DEC_SEQ)
    row, col = (pos // GRID_W).astype(np.float64), (pos % GRID_W).astype(np.float64)
    ang = np.concatenate([np.tile(row[:, None] * inv[None, :], (1, 2)),
                          np.tile(col[:, None] * inv[None, :], (1, 2))], axis=1)
    sign = np.tile(np.concatenate([-np.ones(f), np.ones(f)]), 2)[None, :]
    cos = np.concatenate([np.ones((TM, HEAD_DIM)), np.cos(ang)], axis=0)
    sin = np.concatenate([np.zeros((TM, HEAD_DIM)), np.sin(ang) * sign], axis=0)
    rep = W_KV // HEAD_DIM
    return (jnp.asarray(np.tile(cos, (1, rep)).astype(np.float32)),
            jnp.asarray(np.tile(sin, (1, rep)).astype(np.float32)))


def _group_sum_matrix():
    idx = np.arange(W_KV) // HEAD_DIM
    return jnp.asarray((idx[:, None] == idx[None, :]).astype(np.float32)).astype(bf16)


def _qk_prep_kernel(q_ref, kv_ref, gq_ref, gk_ref, cos_ref, sin_ref, bd_ref,
                    qt_ref, kn_ref, khm_ref, vthm_ref):
    cos = cos_ref[...]
    sin = sin_ref[...]
    bd = bd_ref[...]
    lane = lax.broadcasted_iota(jnp.int32, (TM, W_KV), 1)
    first_half = jnp.bitwise_and(lane, HEAD_DIM // 2 - 1) < HEAD_DIM // 4
    quarter = HEAD_DIM // 4

    def norm(x, g):
        hi, lo = _split_bf16(x * x)
        ss = _mm(hi, bd) + _mm(lo, bd)
        return x * lax.rsqrt(ss * (1.0 / HEAD_DIM) + EPS) * g

    def rope(x):
        swapped = jnp.where(first_half, pltpu.roll(x, W_KV - quarter, 1), pltpu.roll(x, quarter, 1))
        return x * cos + swapped * sin

    for j in range(W_ATTN // W_KV):
        cs = slice(j * W_KV, (j + 1) * W_KV)
        qs = rope(norm(q_ref[:, cs], gq_ref[...])) * _QK_SCALE
        for hh in range(N_KV_HEADS):
            qt_ref[j * N_KV_HEADS + hh] = qs[:, hh * HEAD_DIM:(hh + 1) * HEAD_DIM].T.astype(bf16)
    kn = norm(kv_ref[:, :W_KV], gk_ref[...])
    kn_ref[...] = kn
    kr = rope(kn)
    v = kv_ref[:, W_KV:]
    for g in range(N_KV_HEADS):
        hs = slice(g * HEAD_DIM, (g + 1) * HEAD_DIM)
        khm_ref[g] = kr[:, hs].astype(bf16)
        vthm_ref[g] = v[:, hs].T.astype(bf16)


def _qk_prep_call(q, kv, q_norm, k_norm, cos_t, sin_t, bd):
    rep = W_KV // HEAD_DIM
    const = lambda shape: pl.BlockSpec(shape, lambda i: (0, 0))
    tab = pl.BlockSpec((TM, W_KV), lambda i: (jnp.where(i < CTX_TILES, 0, 1 + lax.rem(i, LAT_TPS)), 0))
    tile = lambda w: pl.BlockSpec((TM, w), lambda i: (i, 0))
    return pl.pallas_call(
        _qk_prep_kernel,
        grid=(N_TILES,),
        in_specs=[tile(W_ATTN), tile(2 * W_KV), const((1, W_KV)), const((1, W_KV)), tab, tab,
                  const((W_KV, W_KV))],
        out_specs=[pl.BlockSpec((N_Q_HEADS, HEAD_DIM, TM), lambda i: (0, 0, i)), tile(W_KV),
                   pl.BlockSpec((N_KV_HEADS, TM, HEAD_DIM), lambda i: (0, i, 0)),
                   pl.BlockSpec((N_KV_HEADS, HEAD_DIM, TM), lambda i: (0, 0, i))],
        out_shape=[jax.ShapeDtypeStruct((N_Q_HEADS, HEAD_DIM, N_TOK), bf16),
                   jax.ShapeDtypeStruct((N_TOK, W_KV), f32),
                   jax.ShapeDtypeStruct((N_KV_HEADS, N_TOK, HEAD_DIM), bf16),
                   jax.ShapeDtypeStruct((N_KV_HEADS, HEAD_DIM, N_TOK), bf16)],
        compiler_params=_params(1),
        name="qk_prep",
    )(q, kv, jnp.tile(q_norm, rep).reshape(1, W_KV), jnp.tile(k_norm, rep).reshape(1, W_KV),
      cos_t, sin_t, bd)


_ATT_CK = 256


def _attn_kernel(hps, has_cache, seq_len, qt_ref, k_ref, vt_ref, *rest):
    if has_cache:
        kc_ref, vct_ref, o_ref, s_ref = rest
    else:
        o_ref, s_ref = rest
    chunks = ([(True, c) for c in range(PAST_LEN // _ATT_CK)] if has_cache else [])
    chunks += [(False, c) for c in range(seq_len // _ATT_CK)]
    tq = qt_ref.shape[2]
    groups = _ATT_CK // SUBLANES

    def span(c):
        return slice(c * _ATT_CK, (c + 1) * _ATT_CK)

    def k_chunk(g, chunk):
        cached, c = chunk
        return kc_ref[g, span(c), :] if cached else k_ref[g, span(c), :]

    def vt_chunk(g, chunk):
        cached, c = chunk
        return vct_ref[g, :, span(c)] if cached else vt_ref[g, :, span(c)]

    def score_chunk(h, i, m8):
        s_ref[h % 2, span(i), :] = _mm(k_chunk(h // Q_PER_KV, chunks[i]), qt_ref[h])
        return jnp.maximum(m8, jnp.max(s_ref[h % 2, span(i), :].reshape(groups, SUBLANES, tq), axis=0))

    neg = jnp.full((SUBLANES, tq), -jnp.inf, f32)
    m8 = neg
    for i in range(len(chunks)):
        m8 = score_chunk(0, i, m8)
    for h in range(hps):
        m = jnp.max(m8, axis=0, keepdims=True)
        m8 = neg
        l8 = jnp.zeros((SUBLANES, tq), f32)
        acc = jnp.zeros((HEAD_DIM, tq), f32)
        for i in range(len(chunks)):
            if h + 1 < hps:
                m8 = score_chunk(h + 1, i, m8)
            p = jnp.exp2(s_ref[h % 2, span(i), :] - m)
            l8 = l8 + jnp.sum(p.reshape(groups, SUBLANES, tq), axis=0)
            acc = acc + _mm(vt_chunk(h // Q_PER_KV, chunks[i]), p.astype(bf16))
        l = jnp.sum(l8, axis=0, keepdims=True)
        o_ref[:, h * HEAD_DIM:(h + 1) * HEAD_DIM] = (acc / l).T.astype(bf16)


def _attn_call(qt, khm, vthm, n_seq, seq_len, hps, tq, tok0, cache=None):
    kvb = hps // Q_PER_KV
    qtiles = seq_len // tq
    row0 = tok0 // tq
    seq0 = tok0 // seq_len
    in_specs = [
        pl.BlockSpec((hps, HEAD_DIM, tq), lambda b, g, t: (g, 0, row0 + b * qtiles + t)),
        pl.BlockSpec((kvb, seq_len, HEAD_DIM), lambda b, g, t: (g, seq0 + b, 0)),
        pl.BlockSpec((kvb, HEAD_DIM, seq_len), lambda b, g, t: (g, 0, seq0 + b)),
    ]
    args = [qt, khm, vthm]
    if cache is not None:
        in_specs += [pl.BlockSpec((None, kvb, PAST_LEN, HEAD_DIM), lambda b, g, t: (b, g, 0, 0)),
                     pl.BlockSpec((None, kvb, HEAD_DIM, PAST_LEN), lambda b, g, t: (b, g, 0, 0))]
        args += list(cache)
    return pl.pallas_call(
        functools.partial(_attn_kernel, hps, cache is not None, seq_len),
        grid=(n_seq, N_Q_HEADS // hps, qtiles),
        in_specs=in_specs,
        out_specs=pl.BlockSpec((tq, hps * HEAD_DIM), lambda b, g, t: (b * qtiles + t, g)),
        out_shape=jax.ShapeDtypeStruct((n_seq * seq_len, W_ATTN), bf16),
        scratch_shapes=[pltpu.VMEM((2, seq_len + (PAST_LEN if cache is not None else 0), tq), f32)],
        compiler_params=_params(3, VMEM_LIMIT),
        name="attention",
    )(*args)


def _merge_kernel(x_ref, mod_ref, pa_ref, pbc_ref, pbl_ref, pc_ref, attc_ref, attl_ref,
                  sa_ref, sb_ref, sc_ref, sd_ref, gm_ref,
                  woa_ref, wob_ref, woc_ref, wod_ref, wout_ref, o_ref):
    is_ctx = pl.program_id(0) < CTX_TILES

    def gated(p, s_ref):
        return (p.astype(f32) * s_ref[...].astype(f32)).astype(bf16)

    def gm(j):
        return gm_ref[:, j * D_MODEL:(j + 1) * D_MODEL].astype(f32)

    pb = jnp.where(is_ctx, pbc_ref[...], pbl_ref[...])
    att = jnp.where(is_ctx, attc_ref[...], attl_ref[...])
    merged = gm(0) * _mm(gated(pa_ref[...], sa_ref), woa_ref[...])
    merged = merged + gm(1) * _mm(gated(pb, sb_ref), wob_ref[...])
    merged = merged + gm(2) * _mm(gated(pc_ref[...], sc_ref), woc_ref[...])
    merged = merged + gm(3) * _mm(gated(att, sd_ref), wod_ref[...])
    y = _mm(merged.astype(bf16), wout_ref[...])
    o_ref[...] = x_ref[...] + mod_ref[2:3, :] * y


def _merge_call(x, mod_tiles_l, pa, pb_ctx, pb_lat, pc, att_ctx, att_lat, sa, sb, sc, sd, gm,
                woa, wob, woc, wod, wout):
    tile = lambda w: pl.BlockSpec((TM, w), lambda i: (i, 0))
    ctx_tile = lambda w: pl.BlockSpec((TM, w), lambda i: (jnp.minimum(i, CTX_TILES - 1), 0))
    lat_tile = lambda w: pl.BlockSpec((TM, w), lambda i: (jnp.maximum(i - CTX_TILES, 0), 0))
    const = lambda shape: pl.BlockSpec(shape, lambda i: (0, 0))
    return pl.pallas_call(
        _merge_kernel,
        grid=(N_TILES,),
        in_specs=[tile(D_MODEL), pl.BlockSpec((None, 3, D_MODEL), lambda i: (i, 0, 0)),
                  tile(W_BR), ctx_tile(W_BR), lat_tile(W_BR), tile(W_BR),
                  ctx_tile(W_ATTN), lat_tile(W_ATTN),
                  tile(W_BR), tile(W_BR), tile(W_BR), tile(W_ATTN), tile(4 * D_MODEL),
                  const((W_BR, D_MODEL)), const((W_BR, D_MODEL)), const((W_BR, D_MODEL)),
                  const((W_ATTN, D_MODEL)), const((D_MODEL, D_MODEL))],
        out_specs=tile(D_MODEL),
        out_shape=jax.ShapeDtypeStruct((N_TOK, D_MODEL), f32),
        compiler_params=_params(1, VMEM_LIMIT),
        name="merge_out",
    )(x, mod_tiles_l, pa, pb_ctx, pb_lat, pc, att_ctx, att_lat, sa, sb, sc, sd, gm,
      woa, wob, woc, wod, wout)


def kernel(x_prompt, x_sample, cache_k, cache_v, c, c_ctx, w_ada, b_ada, norm_g, w_in, conv_dw_w, conv_dw_b, conv_ln_g, conv_ln_b, conv_pw, hy_short_w, hy_short_b, hy_w1, hy_b1, hy_freq, hy_w2, hy_b2, hy_w3, hy_b3, hy_skip, pool_w, pool_scale, q_norm, k_norm, wo_conv, wo_hyena, wo_pool, wo_attn, w_out):
    x = jnp.concatenate([x_prompt.reshape(N_CTX_TOK, D_MODEL), x_sample.reshape(N_LAT_TOK, D_MODEL)], axis=0)

    cond8 = jnp.concatenate([c_ctx[None, :], c, jnp.zeros((8 - 1 - DEC_BATCH, D_MODEL), f32)], axis=0)
    mod = _ada_call(cond8, w_ada, b_ada)
    tile_cond = np.concatenate([np.zeros(CTX_TILES, np.int32),
                                1 + np.arange(N_TILES - CTX_TILES, dtype=np.int32) // LAT_TPS])
    mod_tiles = mod[:, tile_cond].reshape(DEPTH, N_TILES, 3, D_MODEL)

    seq_lens = (SEQ, DEC_SEQ)
    tabs = {L: _dft_tables(2 * L // DFT_B) for L in seq_lens}
    feats = {L: _filter_features(L) for L in seq_lens}
    absd = _abs_deltas()
    cos_t, sin_t = _rope_tables()
    bd = _group_sum_matrix()
    w1p = jnp.pad(hy_w1, ((0, 0), (0, _FEAT_PAD - FILTER_EMB), (0, 0)))

    ks, vs = [], []
    for l in range(DEPTH):
        (pa, sa, v, x1, x2, sb, pc, sc, q, kv, sd, gm) = _inproj_call(
            x, mod_tiles[l], norm_g[l], w_in[l].astype(bf16), conv_dw_w[l], conv_dw_b[l], conv_ln_g[l],
            conv_ln_b[l], conv_pw[l].astype(bf16), hy_short_w[l], hy_short_b[l], pool_w[l].astype(bf16),
            pool_scale[l])

        spec = {}
        for L in seq_lens:
            hfilt, ss = _filter_mlp_call(L, feats[L][0], feats[L][1], w1p[l], hy_b1[l], hy_freq[l], hy_w2[l],
                                         hy_b2[l], hy_w3[l], hy_b3[l], absd)
            spec[L] = _filter_spectrum(2 * L // DFT_B, tabs[L][0], hfilt, ss)
        pb_ctx = _hy_ctx_call(tabs[SEQ], spec[SEQ], v, x1, x2, hy_skip[l])
        pb_lat = _hy_lat_call(tabs[DEC_SEQ], spec[DEC_SEQ], v, x1, x2, hy_skip[l]).reshape(N_LAT_TOK, W_BR)

        qt, kn, khm, vthm = _qk_prep_call(q, kv, q_norm[l], k_norm[l], cos_t, sin_t, bd)
        att_ctx = _attn_call(qt, khm, vthm, BATCH, SEQ, N_Q_HEADS, SEQ, 0)
        cache = (cache_k[:, l].astype(bf16).transpose(0, 2, 1, 3), cache_v[:, l].astype(bf16).transpose(0, 2, 3, 1))
        att_lat = _attn_call(qt, khm, vthm, DEC_BATCH, DEC_SEQ, Q_PER_KV, 256, N_CTX_TOK, cache)

        x = _merge_call(x, mod_tiles[l], pa, pb_ctx, pb_lat, pc, att_ctx, att_lat, sa, sb, sc, sd, gm,
                        wo_conv[l].astype(bf16), wo_hyena[l].astype(bf16), wo_pool[l].astype(bf16),
                        wo_attn[l].astype(bf16), w_out[l].astype(bf16))

        ks.append(kn[:N_CTX_TOK].reshape(BATCH, SEQ, N_KV_HEADS, HEAD_DIM))
        vs.append(kv[:N_CTX_TOK, W_KV:].reshape(BATCH, SEQ, N_KV_HEADS, HEAD_DIM))

    y_prompt = x[:N_CTX_TOK].reshape(BATCH, SEQ, D_MODEL)
    y_sample = x[N_CTX_TOK:].reshape(DEC_BATCH, DEC_SEQ, D_MODEL)
    return (y_prompt, y_sample, jnp.stack(ks, axis=1), jnp.stack(vs, axis=1))
```

```python
import functools
import math

import numpy as np
import jax
import jax.numpy as jnp
from jax import lax
from jax.experimental import pallas as pl
from jax.experimental.pallas import tpu as pltpu

f32 = jnp.float32
bf16 = jnp.bfloat16

D_MODEL = 1024
BATCH = 16
SEQ = 256
DEPTH = 2
DEC_BATCH = 2
DEC_SEQ = 4096
PAST_LEN = 256
GRID_W = 64
W_BR = 512
CONV_K = 31
SHORT_K = 3
FILTER_BANDS = 16
FILTER_EMB = 1 + 2 * FILTER_BANDS
FILTER_HIDDEN = 64
DECAY_TARGET = 1e-2
FAST_DECAY_PCT = 0.3
SLOW_DECAY_PCT = 1.5
POOL_WINDOWS = (2, 4, 8, 16)
POOL_GROUP = W_BR // len(POOL_WINDOWS)
HEAD_DIM = 64
N_Q_HEADS = 16
N_KV_HEADS = 4
Q_PER_KV = N_Q_HEADS // N_KV_HEADS
W_ATTN = N_Q_HEADS * HEAD_DIM
W_KV = N_KV_HEADS * HEAD_DIM
ROPE_THETA = 10000.0
EPS = 1e-6
N_IN = 2 * W_BR + W_BR + 3 * W_BR + W_BR + W_BR + W_BR + W_ATTN + 2 * W_KV + W_ATTN + 4 * D_MODEL

TM = 256
N_CTX_TOK = BATCH * SEQ
N_LAT_TOK = DEC_BATCH * DEC_SEQ
N_TOK = N_CTX_TOK + N_LAT_TOK
N_TILES = N_TOK // TM
CTX_TILES = N_CTX_TOK // TM
LAT_TPS = DEC_SEQ // TM
DFT_B = 256
LANES = 128
SUBLANES = 8
VMEM_LIMIT = 56 * 1024 * 1024

assert SEQ == TM == DFT_B and CTX_TILES % LAT_TPS == 0


def _sigmoid(x):
    return 1.0 / (1.0 + jnp.exp(-x))


def _silu(x):
    return x * _sigmoid(x)


def _mm(a, b):
    return jnp.dot(a, b, preferred_element_type=f32)


def _split_bf16(a):
    hi = a.astype(bf16)
    lo = (a - hi.astype(f32)).astype(bf16)
    return hi, lo


def _mm3(a, b):
    ah, al = _split_bf16(a)
    bh, bl = _split_bf16(b)
    return _mm(ah, bh) + (_mm(ah, bl) + _mm(al, bh))


def _params(n_axes, vmem=None):
    return pltpu.CompilerParams(dimension_semantics=("arbitrary",) * n_axes,
                                vmem_limit_bytes=vmem)


def _tile_position():
    i = pl.program_id(0)
    return i >= CTX_TILES, lax.rem(i, LAT_TPS)


def _halo_flags():
    is_lat, jt = _tile_position()
    return jnp.logical_and(is_lat, jt > 0), jnp.logical_and(is_lat, jt < LAT_TPS - 1)


def _halo_specs(width, hb):
    r = TM // hb
    last = N_TOK // hb - 1
    return [
        pl.BlockSpec((hb, width), lambda i: (jnp.maximum(i * r - 1, 0), 0)),
        pl.BlockSpec((TM, width), lambda i: (i, 0)),
        pl.BlockSpec((hb, width), lambda i: (jnp.minimum((i + 1) * r, last), 0)),
    ]


def _ada_kernel(c_ref, w_ref, b_ref, o_ref):
    c = c_ref[...]
    o_ref[...] = _mm(_silu(c).astype(bf16), w_ref[...].astype(bf16)) + b_ref[...]


def _ada_call(cond8, w_ada, b_ada):
    nb = 3 * D_MODEL // D_MODEL
    return pl.pallas_call(
        _ada_kernel,
        grid=(DEPTH, nb),
        in_specs=[
            pl.BlockSpec((8, D_MODEL), lambda l, j: (0, 0)),
            pl.BlockSpec((None, D_MODEL, D_MODEL), lambda l, j: (l, 0, j)),
            pl.BlockSpec((None, 1, D_MODEL), lambda l, j: (l, 0, j)),
        ],
        out_specs=pl.BlockSpec((None, 8, D_MODEL), lambda l, j: (l, 0, j)),
        out_shape=jax.ShapeDtypeStruct((DEPTH, 8, 3 * D_MODEL), f32),
        compiler_params=_params(2, VMEM_LIMIT),
        name="ada_mod",
    )(cond8, w_ada, b_ada.reshape(DEPTH, 1, 3 * D_MODEL))


_C_GLU = 0
_C_AGATE = 2 * W_BR
_C_BPROJ = _C_AGATE + W_BR
_C_BGATE = _C_BPROJ + 3 * W_BR
_C_CIN = _C_BGATE + W_BR
_C_CGATE = _C_CIN + W_BR
_C_Q = _C_CGATE + W_BR
_C_KV = _C_Q + W_ATTN
_C_DGATE = _C_KV + 2 * W_KV
_C_GM = _C_DGATE + W_ATTN
assert _C_GM + 4 * D_MODEL == N_IN


_CONF_HB = 16
_CONF_CHUNK = 32
_CONF_SPAN = TM + 2 * _CONF_HB - SUBLANES
assert (CONV_K - 1 + _CONF_HB - CONV_K // 2) // SUBLANES * SUBLANES + TM <= _CONF_SPAN


def _conformer_tasks(pad_ref, shift_ref, conv_ref, dw_ref, dwb_ref, lng_ref, lnb_ref, pw_ref, o_ref):
    def shifted_copies():
        for r in range(SUBLANES):
            shift_ref[r] = pad_ref[r:r + _CONF_SPAN, :]

    def conv_chunk(c):
        r0 = c * _CONF_CHUNK
        off = _CONF_HB - CONV_K // 2
        groups = _CONF_CHUNK // SUBLANES
        acc = jnp.broadcast_to(dwb_ref[...][None], (groups, SUBLANES, W_BR))
        for j in range(CONV_K):
            a0 = r0 + (j + off) // SUBLANES * SUBLANES
            x = shift_ref[(j + off) % SUBLANES, a0:a0 + _CONF_CHUNK, :]
            acc = acc + dw_ref[j][None] * x.reshape(groups, SUBLANES, W_BR)
        conv_ref[r0:r0 + _CONF_CHUNK, :] = acc.reshape(_CONF_CHUNK, W_BR)

    def finish():
        a = conv_ref[...]
        mu = jnp.mean(a, axis=-1, keepdims=True)
        xc = a - mu
        var = jnp.mean(xc * xc, axis=-1, keepdims=True)
        y = xc * lax.rsqrt(var + EPS) * lng_ref[...] + lnb_ref[...]
        o_ref[...] = _mm(_silu(y).astype(bf16), pw_ref[...]).astype(bf16)

    return ([shifted_copies] + [functools.partial(conv_chunk, c) for c in range(TM // _CONF_CHUNK)]
            + [finish])


def _short_conv_tasks(pad_ref, w_ref, b_ref, outs):
    def group(g):
        cs = slice(g * W_BR, (g + 1) * W_BR)
        acc = jnp.broadcast_to(b_ref[:, cs], (TM, W_BR))
        for j in range(SHORT_K):
            r0 = _CONF_HB + j - SHORT_K // 2
            acc = acc + w_ref[j:j + 1, cs] * pad_ref[r0:r0 + TM, cs]
        outs[g][...] = acc

    return [functools.partial(group, g) for g in range(len(outs))]


def _pool_tasks(pad_ref, pw_ref, ps_ref, o_ref):
    def group(g):
        is_lat, jt = _tile_position()
        seq_len = jnp.where(is_lat, DEC_SEQ, SEQ)
        t = jnp.where(is_lat, jt * TM, 0) + lax.broadcasted_iota(jnp.int32, (TM, POOL_GROUP), 0)
        cs = slice(g * POOL_GROUP, (g + 1) * POOL_GROUP)
        hw = POOL_WINDOWS[g] // 2
        s = pad_ref[_CONF_HB - hw:_CONF_HB - hw + TM, cs]
        for d in range(-hw + 1, hw):
            s = s + pad_ref[_CONF_HB + d:_CONF_HB + d + TM, cs]
        cnt = (jnp.minimum(t + hw, seq_len) - jnp.maximum(t - hw, 0)).astype(f32)
        pooled = s / cnt - pad_ref[_CONF_HB:_CONF_HB + TM, cs]
        y = _mm(pooled.astype(bf16), pw_ref[g])
        o_ref[:, cs] = (y * ps_ref[:, cs]).astype(bf16)

    return [functools.partial(group, g) for g in range(len(POOL_WINDOWS))]


def _inproj_kernel(xp_ref, x_ref, xn_ref, mod_ref, g_ref, w_ref,
                   dw_ref, dwb_ref, lng_ref, lnb_ref, cpw_ref, sw_ref, sbias_ref, ppw_ref, pps_ref,
                   pa_ref, sa_ref, v_ref, x1_ref, x2_ref, sb_ref, pc_ref, sc_ref, q_ref, kv_ref, sd_ref, gm_ref,
                   pad_a, shift_a, conv_a, pad_b, pad_c):
    has_prev, has_next = _halo_flags()
    gain = g_ref[...]
    scale1 = 1.0 + mod_ref[1:2, :]
    shift = mod_ref[0:1, :]

    def modulated(x):
        ms = jnp.mean(x * x, axis=-1, keepdims=True)
        return (x * lax.rsqrt(ms + EPS) * gain * scale1 + shift).astype(bf16)

    h = modulated(x_ref[...])
    h_ext = jnp.concatenate([modulated(xp_ref[...]), h, modulated(xn_ref[...])], axis=0)
    rows = TM + 2 * _CONF_HB
    r = lax.broadcasted_iota(jnp.int32, (rows, 1), 0)
    valid = jnp.logical_and(jnp.logical_or(r >= _CONF_HB, has_prev),
                            jnp.logical_or(r < _CONF_HB + TM, has_next))

    def proj(c0, width):
        return _mm(h, w_ref[:, c0:c0 + width])

    def proj_ext(c0, width):
        return _mm(h_ext, w_ref[:, c0:c0 + width])

    def ext_into(dst, c0, col):
        def task():
            dst[:, col * W_BR:(col + 1) * W_BR] = jnp.where(valid, proj_ext(c0, W_BR), 0.0)
        return task

    def act_into(dst, c0, col, act, width=W_BR):
        def task():
            y = proj(c0, width)
            dst[:, col * W_BR:col * W_BR + width] = (y if act is None else act(y)).astype(dst.dtype)
        return task

    pad_a[...] = jnp.where(valid, proj_ext(_C_GLU, W_BR) * _sigmoid(proj_ext(_C_GLU + W_BR, W_BR)), 0.0)

    mxu_tasks = [ext_into(pad_b, _C_BPROJ + j * W_BR, j) for j in range(3)] + [ext_into(pad_c, _C_CIN, 0)]
    mxu_tasks += [act_into(sa_ref, _C_AGATE, 0, _silu), act_into(sb_ref, _C_BGATE, 0, _silu),
                  act_into(sc_ref, _C_CGATE, 0, _silu)]
    mxu_tasks += [act_into(q_ref, _C_Q + j * W_BR, j, None) for j in range(W_ATTN // W_BR)]
    mxu_tasks += [act_into(kv_ref, _C_KV, 0, None, 2 * W_KV)]
    mxu_tasks += [act_into(sd_ref, _C_DGATE + j * W_BR, j, _silu) for j in range(W_ATTN // W_BR)]
    mxu_tasks += [act_into(gm_ref, _C_GM + j * W_BR, j, _sigmoid) for j in range(4 * D_MODEL // W_BR)]

    vpu_tasks = _conformer_tasks(pad_a, shift_a, conv_a, dw_ref, dwb_ref, lng_ref, lnb_ref, cpw_ref, pa_ref)
    n_ext = 4
    later = (_short_conv_tasks(pad_b, sw_ref, sbias_ref, (v_ref, x1_ref, x2_ref))
             + _pool_tasks(pad_c, ppw_ref, pps_ref, pc_ref))
    vpu_tasks = vpu_tasks[:n_ext + 1] + later + vpu_tasks[n_ext + 1:]
    for k in range(max(len(mxu_tasks), len(vpu_tasks))):
        if k < len(mxu_tasks):
            mxu_tasks[k]()
        if k < len(vpu_tasks):
            vpu_tasks[k]()


def _inproj_call(x, mod_tiles_l, norm_g_l, w_in_l, dw_w, dw_b, ln_g, ln_b, conv_pw_bf, short_w, short_b,
                 pool_w_bf, pool_scale):
    widths = [(W_BR, bf16), (W_BR, bf16), (W_BR, f32), (W_BR, f32), (W_BR, f32), (W_BR, bf16),
              (W_BR, bf16), (W_BR, bf16), (W_ATTN, f32), (2 * W_KV, f32), (W_ATTN, bf16), (4 * D_MODEL, bf16)]
    row = lambda v: v.reshape(1, -1)
    rep = lambda v: jnp.broadcast_to(v[..., None, :], v.shape[:-1] + (SUBLANES, W_BR))
    const = lambda shape: pl.BlockSpec(shape, lambda i: (0,) * len(shape))
    ext_rows = TM + 2 * _CONF_HB
    return pl.pallas_call(
        _inproj_kernel,
        grid=(N_TILES,),
        in_specs=_halo_specs(D_MODEL, _CONF_HB) + [
            pl.BlockSpec((None, 3, D_MODEL), lambda i: (i, 0, 0)),
            const((1, D_MODEL)),
            pl.BlockSpec((D_MODEL, N_IN), lambda i: (0, 0), pipeline_mode=pl.Buffered(1)),
            const((CONV_K, SUBLANES, W_BR)), const((SUBLANES, W_BR)), const((1, W_BR)), const((1, W_BR)),
            const((W_BR, W_BR)), const((SHORT_K, 3 * W_BR)), const((1, 3 * W_BR)),
            const((len(POOL_WINDOWS), POOL_GROUP, POOL_GROUP)), const((1, W_BR))],
        out_specs=[pl.BlockSpec((TM, w), lambda i: (i, 0)) for w, _ in widths],
        out_shape=[jax.ShapeDtypeStruct((N_TOK, w), dt) for w, dt in widths],
        scratch_shapes=[pltpu.VMEM((ext_rows, W_BR), f32), pltpu.VMEM((SUBLANES, _CONF_SPAN, W_BR), f32),
                        pltpu.VMEM((TM, W_BR), f32), pltpu.VMEM((ext_rows, 3 * W_BR), f32),
                        pltpu.VMEM((ext_rows, W_BR), f32)],
        compiler_params=_params(1, VMEM_LIMIT),
        name="in_proj",
    )(x, x, x, mod_tiles_l, row(norm_g_l), w_in_l, rep(dw_w), rep(dw_b), row(ln_g), row(ln_b), conv_pw_bf,
      short_w, row(short_b), pool_w_bf, row(pool_scale))


def _dft_tables(n2):
    n = DFT_B * n2
    k1 = np.arange(DFT_B, dtype=np.int64)[:, None]
    t1 = np.arange(DFT_B, dtype=np.int64)[None, :]
    fwd, inv = [], []
    for k2 in range(n2):
        ang = -2.0 * np.pi * (((n2 * k1 + k2) * t1) % n).astype(np.float64) / n
        gr, gi = np.cos(ang), np.sin(ang)
        fwd.append(np.concatenate([gr, gi], axis=0))
        inv.append(np.concatenate([gr.T, gi.T], axis=1))
    return (jnp.asarray(np.stack(fwd).astype(np.float32)).astype(bf16),
            jnp.asarray(np.stack(inv).astype(np.float32)).astype(bf16))


def _cmul_const(x, ang):
    c, s = math.cos(ang), math.sin(ang)
    xr, xi = x
    return (c * xr - s * xi, s * xr + c * xi)


def _fft_list(xs, sign):
    n = len(xs)
    if n == 1:
        return list(xs)
    ev = _fft_list(xs[0::2], sign)
    od = _fft_list(xs[1::2], sign)
    out = [None] * n
    for k in range(n // 2):
        orr, oi = od[k]
        if k == 0:
            tr, ti = orr, oi
        elif 4 * k == n:
            tr, ti = (-oi, orr) if sign > 0 else (oi, -orr)
        else:
            tr, ti = _cmul_const(od[k], sign * 2.0 * math.pi * k / n)
        er, ei = ev[k]
        out[k] = (er + tr, ei + ti)
        out[k + n // 2] = (er - tr, ei - ti)
    return out


def _fft_padded(xs, sign):
    m = len(xs)
    ev = _fft_list(xs, sign)
    od = _fft_list([xs[0]] + [_cmul_const(xs[t], sign * math.pi * t / m) for t in range(1, m)], sign)
    out = []
    for j in range(m):
        out += [ev[j], od[j]]
    return out


def _ifft_truncated(ws):
    m = len(ws) // 2
    ev = _fft_list(ws[0::2], 1)
    od = _fft_list(ws[1::2], 1)
    out = []
    for t in range(m):
        tr, ti = od[t] if t == 0 else _cmul_const(od[t], math.pi * t / m)
        out.append((ev[t][0] + tr, ev[t][1] + ti))
    return out


def _spectral_block(u, m, mt, hr, hi):
    w = hr.shape[1]
    hr, hi = hr.astype(f32), hi.astype(f32)
    p = _mm(m, u)
    xr = p[:DFT_B, :w] - p[DFT_B:, w:]
    xi = p[:DFT_B, w:] + p[DFT_B:, :w]
    yr = xr * hr - xi * hi
    yi = xr * hi + xi * hr
    rhs = jnp.concatenate([jnp.concatenate([yr, yi], axis=1),
                           jnp.concatenate([yi, -yr], axis=1)], axis=0).astype(bf16)
    return _mm(mt, rhs)


def _slab(s):
    return pl.ds(pl.multiple_of(s * SUBLANES, SUBLANES), SUBLANES)


_FEAT_PAD = 64


def _filter_features(seq_len):
    m = np.arange(2 * seq_len)
    j = np.where(m < seq_len, m, 2 * seq_len - m).astype(np.float64)
    t = j / (seq_len - 1)
    bands = np.linspace(1e-4, FILTER_BANDS - 1, FILTER_BANDS)[None, :]
    w = (2.0 * math.pi / seq_len) * j[:, None]
    z = np.concatenate([t[:, None], np.cos(bands * w), np.sin(bands * w)], axis=-1)
    z = np.pad(z, ((0, 0), (0, _FEAT_PAD - FILTER_EMB)))
    return jnp.asarray(z.astype(np.float32)), jnp.asarray(t.astype(np.float32)[:, None])


def _abs_deltas():
    max_decay = math.log(DECAY_TARGET) / FAST_DECAY_PCT
    min_decay = math.log(DECAY_TARGET) / SLOW_DECAY_PCT
    d = np.abs(np.linspace(min_decay, max_decay, W_BR))
    return jnp.asarray(np.concatenate([d, d]).astype(np.float32)[None, :])


def _filter_mlp_kernel(seq_len, tr, z_ref, t_ref, w1_ref, b1_ref, fr_ref, w2_ref, b2_ref, w3_ref, b3_ref,
                       ad_ref, h_ref, ss_ref):
    i = pl.program_id(0)
    fr = fr_ref[...]
    hdn = jnp.sin(fr * (_mm3(z_ref[...], w1_ref[...]) + b1_ref[...]))
    hdn = jnp.sin(fr * (_mm3(hdn, w2_ref[...]) + b2_ref[...]))
    h = _mm3(hdn, w3_ref[...]) + b3_ref[...]
    h = h * jnp.exp(-t_ref[...] * ad_ref[...])
    m = i * tr + lax.broadcasted_iota(jnp.int32, h.shape, 0)
    h = jnp.where(m == seq_len, 0.0, h)
    h_ref[...] = h

    @pl.when(i == 0)
    def _():
        ss_ref[...] = jnp.zeros_like(ss_ref)

    ss_ref[...] += jnp.sum(h * h, axis=0, keepdims=True)


def _filter_mlp_call(seq_len, feats, tcol, w1p, b1, freq, w2, b2, w3, b3, absd):
    tr = min(512, seq_len)
    steps = 2 * seq_len // tr
    half_steps = seq_len // tr
    wide = 2 * W_BR
    const = lambda shape: pl.BlockSpec(shape, lambda i: (0, 0))
    row = lambda v: v.reshape(1, -1)
    return pl.pallas_call(
        functools.partial(_filter_mlp_kernel, seq_len, tr),
        grid=(steps,),
        in_specs=[
            pl.BlockSpec((tr, _FEAT_PAD), lambda i: (i, 0)),
            pl.BlockSpec((tr, 1), lambda i: (i, 0)),
            const((_FEAT_PAD, FILTER_HIDDEN)), const((1, FILTER_HIDDEN)), const((1, FILTER_HIDDEN)),
            const((FILTER_HIDDEN, FILTER_HIDDEN)), const((1, FILTER_HIDDEN)),
            pl.BlockSpec((FILTER_HIDDEN, wide), lambda i: (0, i // half_steps)),
            pl.BlockSpec((1, wide), lambda i: (0, i // half_steps)),
            const((1, wide)),
        ],
        out_specs=[pl.BlockSpec((tr, wide), lambda i: (i, 0)), const((1, wide))],
        out_shape=[jax.ShapeDtypeStruct((2 * seq_len, wide), f32), jax.ShapeDtypeStruct((1, wide), f32)],
        compiler_params=_params(1),
        name="filter_mlp",
    )(feats, tcol, w1p, row(b1), row(freq), w2, row(b2), w3, row(b3), absd)


_RADIX_ROWS = 128


def _filter_radix_kernel(n2, h_ref, ur_ref, ui_ref, sr_ref, si_ref):
    zero = jnp.zeros((SUBLANES, LANES), f32)

    def body(s, carry):
        rs = _slab(s)
        us = _fft_list([(h_ref[t, rs, :], zero) for t in range(n2)], -1)
        for k in range(n2):
            sr_ref[k, rs, :] = us[k][0]
            si_ref[k, rs, :] = us[k][1]
        return carry

    lax.fori_loop(0, _RADIX_ROWS // SUBLANES, body, 0)
    for k in range(n2):
        ur_ref[k] = sr_ref[k].astype(bf16)
        ui_ref[k] = si_ref[k].astype(bf16)


def _filter_spec_kernel(scale_const, ur_ref, ui_ref, m_ref, ss_ref, hr_ref, hi_ref):
    w = ur_ref.shape[1]
    p = _mm(m_ref[...], jnp.concatenate([ur_ref[...], ui_ref[...]], axis=1))
    scale = lax.rsqrt(ss_ref[...] + EPS) * scale_const
    hr_ref[...] = ((p[:DFT_B, :w] - p[DFT_B:, w:]) * scale).astype(hr_ref.dtype)
    hi_ref[...] = ((p[:DFT_B, w:] + p[DFT_B:, :w]) * scale).astype(hi_ref.dtype)


def _filter_spectrum(n2, mfwd, hfilt, ss):
    wide = 2 * W_BR
    blk = pl.BlockSpec((n2, _RADIX_ROWS, LANES), lambda r, c: (0, r, c))
    u_shape = jax.ShapeDtypeStruct((n2, DFT_B, wide), bf16)
    ur, ui = pl.pallas_call(
        functools.partial(_filter_radix_kernel, n2),
        grid=(DFT_B // _RADIX_ROWS, wide // LANES),
        in_specs=[blk],
        out_specs=[blk, blk],
        out_shape=[u_shape, u_shape],
        scratch_shapes=[pltpu.VMEM((n2, _RADIX_ROWS, LANES), f32)] * 2,
        compiler_params=_params(2),
        name="filter_radix",
    )(hfilt.reshape(n2, DFT_B, wide))
    kspec = pl.BlockSpec((None, DFT_B, wide), lambda k: (k, 0, 0))
    h_shape = jax.ShapeDtypeStruct((n2, DFT_B, wide), bf16)
    return pl.pallas_call(
        functools.partial(_filter_spec_kernel, 1.0 / (n2 * DFT_B)),
        grid=(n2,),
        in_specs=[kspec, kspec, pl.BlockSpec((None, 2 * DFT_B, DFT_B), lambda k: (k, 0, 0)),
                  pl.BlockSpec((1, wide), lambda k: (0, 0))],
        out_specs=[kspec, kspec],
        out_shape=[h_shape, h_shape],
        compiler_params=_params(1),
        name="filter_spec",
    )(ur, ui, mfwd, ss)


def _hy_ctx_kernel(va_ref, vb_ref, x1a_ref, x1b_ref, x2a_ref, x2b_ref, m_ref, mt_ref, hr_ref, hi_ref,
                   skip_ref, o_ref):
    n2 = m_ref.shape[0]
    za, zb = va_ref[...], vb_ref[...]
    gates = ((x1a_ref, x1b_ref), (x2a_ref, x2b_ref))
    for order in range(2):
        cs = slice(order * W_BR, (order + 1) * W_BR)
        u = jnp.concatenate([za, zb], axis=1).astype(bf16)
        w = _spectral_block(u, m_ref[0], mt_ref[0], hr_ref[0, :, cs], hi_ref[0, :, cs])
        for k2 in range(1, n2):
            w = w + _spectral_block(u, m_ref[k2], mt_ref[k2], hr_ref[k2, :, cs], hi_ref[k2, :, cs])
        skip = skip_ref[order:order + 1, :]
        za = gates[order][0][...] * (w[:, :W_BR] + za * skip)
        zb = gates[order][1][...] * (w[:, W_BR:] + zb * skip)
    o_ref[0:DFT_B, :] = za
    o_ref[DFT_B:, :] = zb


def _hy_ctx_call(tabs, spec, v, x1, x2, skip):
    mfwd, minv = tabs
    n2 = mfwd.shape[0]
    assert n2 == 2
    seq = lambda par: pl.BlockSpec((DFT_B, W_BR), lambda p: (2 * p + par, 0))
    whole = lambda a: pl.BlockSpec(a.shape, lambda p: (0,) * a.ndim)
    return pl.pallas_call(
        _hy_ctx_kernel,
        grid=(BATCH // 2,),
        in_specs=[seq(0), seq(1), seq(0), seq(1), seq(0), seq(1), whole(mfwd), whole(minv),
                  whole(spec[0]), whole(spec[1]), whole(skip)],
        out_specs=pl.BlockSpec((2 * DFT_B, W_BR), lambda p: (p, 0)),
        out_shape=jax.ShapeDtypeStruct((N_CTX_TOK, W_BR), f32),
        compiler_params=_params(1, VMEM_LIMIT),
        name="hyena_ctx",
    )(v, v, x1, x1, x2, x2, mfwd, minv, spec[0], spec[1], skip)


def _store_radix(us, rs, sr_ref, si_ref):
    for k, (ur, ui) in enumerate(us):
        sr_ref[k, rs, :] = ur
        si_ref[k, rs, :] = ui


def _cast_radix(sr_ref, si_ref, ur_ref, ui_ref):
    for k in range(sr_ref.shape[0]):
        ur_ref[k] = sr_ref[k].astype(bf16)
        ui_ref[k] = si_ref[k].astype(bf16)


def _hy_fwd_radix_kernel(m, za_ref, zb_ref, ur_ref, ui_ref, sr_ref, si_ref):
    def body(s, carry):
        rs = _slab(s)
        _store_radix(_fft_padded([(za_ref[t, rs, :], zb_ref[t, rs, :]) for t in range(m)], -1),
                     rs, sr_ref, si_ref)
        return carry

    lax.fori_loop(0, _RADIX_ROWS // SUBLANES, body, 0)
    _cast_radix(sr_ref, si_ref, ur_ref, ui_ref)


def _hy_spec_kernel(ur_ref, ui_ref, m_ref, mt_ref, hr_ref, hi_ref, w_ref):
    u = jnp.concatenate([ur_ref[...], ui_ref[...]], axis=1)
    w_ref[...] = _spectral_block(u, m_ref[...], mt_ref[...], hr_ref[...], hi_ref[...]).astype(w_ref.dtype)


def _hy_inv_radix_kernel(m, final, wr_ref, wi_ref, za_ref, zb_ref, ga_ref, gb_ref, skip_ref, o_ref, *rest):
    skip = skip_ref[...]
    sr_ref, si_ref = rest[-2:]
    for k in range(2 * m):
        sr_ref[k] = wr_ref[k].astype(f32)
        si_ref[k] = wi_ref[k].astype(f32)

    def body(s, carry):
        rs = _slab(s)
        ys = _ifft_truncated([(sr_ref[k, rs, :], si_ref[k, rs, :]) for k in range(2 * m)])
        zs = []
        for t in range(m):
            za = ga_ref[t, rs, :] * (ys[t][0] + za_ref[t, rs, :] * skip)
            zb = gb_ref[t, rs, :] * (ys[t][1] + zb_ref[t, rs, :] * skip)
            o_ref[0, t, rs, :] = za
            o_ref[1, t, rs, :] = zb
            zs.append((za, zb))
        if not final:
            _store_radix(_fft_padded(zs, -1), rs, sr_ref, si_ref)
        return carry

    lax.fori_loop(0, _RADIX_ROWS // SUBLANES, body, 0)
    if not final:
        _cast_radix(sr_ref, si_ref, rest[0], rest[1])


def _hy_lat_call(tabs, spec, v, x1, x2, skip):
    mfwd, minv = tabs
    n2 = mfwd.shape[0]
    m = n2 // 2
    assert m * DFT_B == DEC_SEQ and DEC_BATCH == 2
    seq0 = N_CTX_TOK // DEC_SEQ
    grid = (DFT_B // _RADIX_ROWS, W_BR // LANES)

    def tok(arr, b):
        return (arr.reshape(N_TOK // DEC_SEQ, m, DFT_B, W_BR),
                pl.BlockSpec((None, m, _RADIX_ROWS, LANES), lambda r, c: (seq0 + b, 0, r, c)))

    def pair(arr, b):
        return (arr, pl.BlockSpec((None, m, _RADIX_ROWS, LANES), lambda r, c: (b, 0, r, c)))

    u_spec = pl.BlockSpec((n2, _RADIX_ROWS, LANES), lambda r, c: (0, r, c))
    u_shape = jax.ShapeDtypeStruct((n2, DFT_B, W_BR), bf16)
    z_spec = pl.BlockSpec((2, m, _RADIX_ROWS, LANES), lambda r, c: (0, 0, r, c))
    z_shape = jax.ShapeDtypeStruct((2, m, DFT_B, W_BR), f32)
    scratch = [pltpu.VMEM((n2, _RADIX_ROWS, LANES), f32)] * 2

    def spectral(ur, ui, order):
        kspec = pl.BlockSpec((None, DFT_B, W_BR), lambda k: (k, 0, 0))
        hspec = pl.BlockSpec((None, DFT_B, W_BR), lambda k: (k, 0, order))
        return pl.pallas_call(
            _hy_spec_kernel,
            grid=(n2,),
            in_specs=[kspec, kspec, pl.BlockSpec((None, 2 * DFT_B, DFT_B), lambda k: (k, 0, 0)),
                      pl.BlockSpec((None, DFT_B, 2 * DFT_B), lambda k: (k, 0, 0)), hspec, hspec],
            out_specs=pl.BlockSpec((None, DFT_B, 2 * W_BR), lambda k: (k, 0, 0)),
            out_shape=jax.ShapeDtypeStruct((n2, DFT_B, 2 * W_BR), bf16),
            compiler_params=_params(1),
            name="hyena_spec",
        )(ur, ui, mfwd, minv, spec[0], spec[1])

    def inverse(w, z_srcs, gate, order, final):
        wr_spec = pl.BlockSpec((n2, _RADIX_ROWS, LANES), lambda r, c: (0, r, c))
        wi_spec = pl.BlockSpec((n2, _RADIX_ROWS, LANES), lambda r, c: (0, r, W_BR // LANES + c))
        g_srcs = [tok(gate, 0), tok(gate, 1)]
        return pl.pallas_call(
            functools.partial(_hy_inv_radix_kernel, m, final),
            grid=grid,
            in_specs=[wr_spec, wi_spec, z_srcs[0][1], z_srcs[1][1], g_srcs[0][1], g_srcs[1][1],
                      pl.BlockSpec((1, LANES), lambda r, c: (0, c))],
            out_specs=z_spec if final else [z_spec, u_spec, u_spec],
            out_shape=z_shape if final else [z_shape, u_shape, u_shape],
            scratch_shapes=scratch,
            compiler_params=_params(2, VMEM_LIMIT),
            name="hyena_inv_radix",
        )(w, w, z_srcs[0][0], z_srcs[1][0], g_srcs[0][0], g_srcs[1][0], skip[order].reshape(1, W_BR))

    v_srcs = [tok(v, 0), tok(v, 1)]
    ur, ui = pl.pallas_call(
        functools.partial(_hy_fwd_radix_kernel, m),
        grid=grid,
        in_specs=[v_srcs[0][1], v_srcs[1][1]],
        out_specs=[u_spec, u_spec],
        out_shape=[u_shape, u_shape],
        scratch_shapes=scratch,
        compiler_params=_params(2),
        name="hyena_fwd_radix",
    )(v_srcs[0][0], v_srcs[1][0])
    z1, ur, ui = inverse(spectral(ur, ui, 0), v_srcs, x1, 0, False)
    return inverse(spectral(ur, ui, 1), [pair(z1, 0), pair(z1, 1)], x2, 1, True)


_QK_SCALE = HEAD_DIM ** -0.5 * math.log2(math.e)


def _rope_tables():
    half = HEAD_DIM // 2
    f = half // 2
    inv = ROPE_THETA ** (-np.arange(f, dtype=np.float64) / f)
    pos = np.arange(DEC_SEQ)
    row, col = (pos // GRID_W).astype(np.float64), (pos % GRID_W).astype(np.float64)
    ang = np.concatenate([np.tile(row[:, None] * inv[None, :], (1, 2)),
                          np.tile(col[:, None] * inv[None, :], (1, 2))], axis=1)
    sign = np.tile(np.concatenate([-np.ones(f), np.ones(f)]), 2)[None, :]
    cos = np.concatenate([np.ones((TM, HEAD_DIM)), np.cos(ang)], axis=0)
    sin = np.concatenate([np.zeros((TM, HEAD_DIM)), np.sin(ang) * sign], axis=0)
    rep = W_KV // HEAD_DIM
    return (jnp.asarray(np.tile(cos, (1, rep)).astype(np.float32)),
            jnp.asarray(np.tile(sin, (1, rep)).astype(np.float32)))


def _group_sum_matrix():
    idx = np.arange(W_KV) // HEAD_DIM
    return jnp.asarray((idx[:, None] == idx[None, :]).astype(np.float32)).astype(bf16)


def _qk_prep_kernel(q_ref, kv_ref, gq_ref, gk_ref, cos_ref, sin_ref, bd_ref,
                    qt_ref, kn_ref, khm_ref, vthm_ref):
    cos = cos_ref[...]
    sin = sin_ref[...]
    bd = bd_ref[...]
    eye = (lax.broadcasted_iota(jnp.int32, (HEAD_DIM, HEAD_DIM), 0)
           == lax.broadcasted_iota(jnp.int32, (HEAD_DIM, HEAD_DIM), 1)).astype(f32).astype(bf16)
    lane = lax.broadcasted_iota(jnp.int32, (TM, W_KV), 1)
    first_half = jnp.bitwise_and(lane, HEAD_DIM // 2 - 1) < HEAD_DIM // 4
    quarter = HEAD_DIM // 4

    def norm(x, g):
        hi, lo = _split_bf16(x * x)
        ss = _mm(hi, bd) + _mm(lo, bd)
        return x * lax.rsqrt(ss * (1.0 / HEAD_DIM) + EPS) * g

    def transposed(x):
        return lax.dot_general(eye, x.astype(bf16), (((1,), (1,)), ((), ())),
                               preferred_element_type=f32).astype(bf16)

    def rope(x):
        swapped = jnp.where(first_half, pltpu.roll(x, W_KV - quarter, 1), pltpu.roll(x, quarter, 1))
        return x * cos + swapped * sin

    for j in range(W_ATTN // W_KV):
        cs = slice(j * W_KV, (j + 1) * W_KV)
        qs = rope(norm(q_ref[:, cs], gq_ref[...])) * _QK_SCALE
        for hh in range(N_KV_HEADS):
            qt_ref[j * N_KV_HEADS + hh] = transposed(qs[:, hh * HEAD_DIM:(hh + 1) * HEAD_DIM])
    kn = norm(kv_ref[:, :W_KV], gk_ref[...])
    kn_ref[...] = kn
    kr = rope(kn)
    v = kv_ref[:, W_KV:]
    for g in range(N_KV_HEADS):
        hs = slice(g * HEAD_DIM, (g + 1) * HEAD_DIM)
        khm_ref[g] = kr[:, hs].astype(bf16)
        vthm_ref[g] = transposed(v[:, hs])


def _qk_prep_call(q, kv, q_norm, k_norm, cos_t, sin_t, bd):
    rep = W_KV // HEAD_DIM
    const = lambda shape: pl.BlockSpec(shape, lambda i: (0, 0))
    tab = pl.BlockSpec((TM, W_KV), lambda i: (jnp.where(i < CTX_TILES, 0, 1 + lax.rem(i, LAT_TPS)), 0))
    tile = lambda w: pl.BlockSpec((TM, w), lambda i: (i, 0))
    return pl.pallas_call(
        _qk_prep_kernel,
        grid=(N_TILES,),
        in_specs=[tile(W_ATTN), tile(2 * W_KV), const((1, W_KV)), const((1, W_KV)), tab, tab,
                  const((W_KV, W_KV))],
        out_specs=[pl.BlockSpec((N_Q_HEADS, HEAD_DIM, TM), lambda i: (0, 0, i)), tile(W_KV),
                   pl.BlockSpec((N_KV_HEADS, TM, HEAD_DIM), lambda i: (0, i, 0)),
                   pl.BlockSpec((N_KV_HEADS, HEAD_DIM, TM), lambda i: (0, 0, i))],
        out_shape=[jax.ShapeDtypeStruct((N_Q_HEADS, HEAD_DIM, N_TOK), bf16),
                   jax.ShapeDtypeStruct((N_TOK, W_KV), f32),
                   jax.ShapeDtypeStruct((N_KV_HEADS, N_TOK, HEAD_DIM), bf16),
                   jax.ShapeDtypeStruct((N_KV_HEADS, HEAD_DIM, N_TOK), bf16)],
        compiler_params=_params(1),
        name="qk_prep",
    )(q, kv, jnp.tile(q_norm, rep).reshape(1, W_KV), jnp.tile(k_norm, rep).reshape(1, W_KV),
      cos_t, sin_t, bd)


_ATT_CK = 256


def _attn_kernel(hps, has_cache, seq_len, qt_ref, k_ref, vt_ref, *rest):
    if has_cache:
        kc_ref, vct_ref, o_ref, s_ref = rest
    else:
        o_ref, s_ref = rest
    chunks = ([(True, c) for c in range(PAST_LEN // _ATT_CK)] if has_cache else [])
    chunks += [(False, c) for c in range(seq_len // _ATT_CK)]
    tq = qt_ref.shape[2]
    groups = _ATT_CK // SUBLANES

    def span(c):
        return slice(c * _ATT_CK, (c + 1) * _ATT_CK)

    def k_chunk(g, chunk):
        cached, c = chunk
        return kc_ref[g, span(c), :] if cached else k_ref[g, span(c), :]

    def vt_chunk(g, chunk):
        cached, c = chunk
        return vct_ref[g, :, span(c)] if cached else vt_ref[g, :, span(c)]

    def score_chunk(h, i, m8):
        s_ref[h % 2, span(i), :] = _mm(k_chunk(h // Q_PER_KV, chunks[i]), qt_ref[h])
        return jnp.maximum(m8, jnp.max(s_ref[h % 2, span(i), :].reshape(groups, SUBLANES, tq), axis=0))

    neg = jnp.full((SUBLANES, tq), -jnp.inf, f32)
    m8 = neg
    for i in range(len(chunks)):
        m8 = score_chunk(0, i, m8)
    for h in range(hps):
        m = jnp.max(m8, axis=0, keepdims=True)
        m8 = neg
        l8 = jnp.zeros((SUBLANES, tq), f32)
        acc = jnp.zeros((HEAD_DIM, tq), f32)
        for i in range(len(chunks)):
            if h + 1 < hps:
                m8 = score_chunk(h + 1, i, m8)
            p = jnp.exp2(s_ref[h % 2, span(i), :] - m)
            l8 = l8 + jnp.sum(p.reshape(groups, SUBLANES, tq), axis=0)
            acc = acc + _mm(vt_chunk(h // Q_PER_KV, chunks[i]), p.astype(bf16))
        l = jnp.sum(l8, axis=0, keepdims=True)
        o_ref[:, h * HEAD_DIM:(h + 1) * HEAD_DIM] = (acc / l).T.astype(bf16)


def _attn_call(qt, khm, vthm, n_seq, seq_len, hps, tq, tok0, cache=None):
    kvb = hps // Q_PER_KV
    qtiles = seq_len // tq
    row0 = tok0 // tq
    seq0 = tok0 // seq_len
    in_specs = [
        pl.BlockSpec((hps, HEAD_DIM, tq), lambda b, g, t: (g, 0, row0 + b * qtiles + t)),
        pl.BlockSpec((kvb, seq_len, HEAD_DIM), lambda b, g, t: (g, seq0 + b, 0)),
        pl.BlockSpec((kvb, HEAD_DIM, seq_len), lambda b, g, t: (g, 0, seq0 + b)),
    ]
    args = [qt, khm, vthm]
    if cache is not None:
        in_specs += [pl.BlockSpec((None, kvb, PAST_LEN, HEAD_DIM), lambda b, g, t: (b, g, 0, 0)),
                     pl.BlockSpec((None, kvb, HEAD_DIM, PAST_LEN), lambda b, g, t: (b, g, 0, 0))]
        args += list(cache)
    return pl.pallas_call(
        functools.partial(_attn_kernel, hps, cache is not None, seq_len),
        grid=(n_seq, N_Q_HEADS // hps, qtiles),
        in_specs=in_specs,
        out_specs=pl.BlockSpec((tq, hps * HEAD_DIM), lambda b, g, t: (b * qtiles + t, g)),
        out_shape=jax.ShapeDtypeStruct((n_seq * seq_len, W_ATTN), bf16),
        scratch_shapes=[pltpu.VMEM((2, seq_len + (PAST_LEN if cache is not None else 0), tq), f32)],
        compiler_params=_params(3, VMEM_LIMIT),
        name="attention",
    )(*args)


def _merge_kernel(x_ref, mod_ref, pa_ref, pbc_ref, pbl_ref, pc_ref, attc_ref, attl_ref,
                  sa_ref, sb_ref, sc_ref, sd_ref, gm_ref,
                  woa_ref, wob_ref, woc_ref, wod_ref, wout_ref, *o_refs):
    is_ctx = pl.program_id(0) < CTX_TILES

    def gated(p, s_ref):
        return (p.astype(f32) * s_ref[...].astype(f32)).astype(bf16)

    def gm(j):
        return gm_ref[:, j * D_MODEL:(j + 1) * D_MODEL].astype(f32)

    pb = jnp.where(is_ctx, pbc_ref[...], pbl_ref[...])
    att = jnp.where(is_ctx, attc_ref[...], attl_ref[...])
    merged = gm(0) * _mm(gated(pa_ref[...], sa_ref), woa_ref[...])
    merged = merged + gm(1) * _mm(gated(pb, sb_ref), wob_ref[...])
    merged = merged + gm(2) * _mm(gated(pc_ref[...], sc_ref), woc_ref[...])
    merged = merged + gm(3) * _mm(gated(att, sd_ref), wod_ref[...])
    y = _mm(merged.astype(bf16), wout_ref[...])
    out = x_ref[...] + mod_ref[2:3, :] * y
    if len(o_refs) == 1:
        o_refs[0][...] = out
    else:
        o_refs[1][...] = out

        @pl.when(is_ctx)
        def _():
            o_refs[0][...] = out


def _merge_call(x, mod_tiles_l, pa, pb_ctx, pb_lat, pc, att_ctx, att_lat, sa, sb, sc, sd, gm,
                woa, wob, woc, wod, wout, split_out):
    tile = lambda w: pl.BlockSpec((TM, w), lambda i: (i, 0))
    ctx_tile = lambda w: pl.BlockSpec((TM, w), lambda i: (jnp.minimum(i, CTX_TILES - 1), 0))
    lat_tile = lambda w: pl.BlockSpec((TM, w), lambda i: (jnp.maximum(i - CTX_TILES, 0), 0))
    const = lambda shape: pl.BlockSpec(shape, lambda i: (0, 0))
    return pl.pallas_call(
        _merge_kernel,
        grid=(N_TILES,),
        in_specs=[tile(D_MODEL), pl.BlockSpec((None, 3, D_MODEL), lambda i: (i, 0, 0)),
                  tile(W_BR), ctx_tile(W_BR), lat_tile(W_BR), tile(W_BR),
                  ctx_tile(W_ATTN), lat_tile(W_ATTN),
                  tile(W_BR), tile(W_BR), tile(W_BR), tile(W_ATTN), tile(4 * D_MODEL),
                  const((W_BR, D_MODEL)), const((W_BR, D_MODEL)), const((W_BR, D_MODEL)),
                  const((W_ATTN, D_MODEL)), const((D_MODEL, D_MODEL))],
        out_specs=[ctx_tile(D_MODEL), lat_tile(D_MODEL)] if split_out else tile(D_MODEL),
        out_shape=([jax.ShapeDtypeStruct((N_CTX_TOK, D_MODEL), f32), jax.ShapeDtypeStruct((N_LAT_TOK, D_MODEL), f32)]
                   if split_out else jax.ShapeDtypeStruct((N_TOK, D_MODEL), f32)),
        compiler_params=_params(1, VMEM_LIMIT),
        name="merge_out",
    )(x, mod_tiles_l, pa, pb_ctx, pb_lat, pc, att_ctx, att_lat, sa, sb, sc, sd, gm,
      woa, wob, woc, wod, wout)


def kernel(x_prompt, x_sample, cache_k, cache_v, c, c_ctx, w_ada, b_ada, norm_g, w_in, conv_dw_w, conv_dw_b, conv_ln_g, conv_ln_b, conv_pw, hy_short_w, hy_short_b, hy_w1, hy_b1, hy_freq, hy_w2, hy_b2, hy_w3, hy_b3, hy_skip, pool_w, pool_scale, q_norm, k_norm, wo_conv, wo_hyena, wo_pool, wo_attn, w_out):
    x = jnp.concatenate([x_prompt.reshape(N_CTX_TOK, D_MODEL), x_sample.reshape(N_LAT_TOK, D_MODEL)], axis=0)

    cond8 = jnp.concatenate([c_ctx[None, :], c, jnp.zeros((8 - 1 - DEC_BATCH, D_MODEL), f32)], axis=0)
    mod = _ada_call(cond8, w_ada, b_ada)
    tile_cond = np.concatenate([np.zeros(CTX_TILES, np.int32),
                                1 + np.arange(N_TILES - CTX_TILES, dtype=np.int32) // LAT_TPS])
    mod_tiles = mod[:, tile_cond].reshape(DEPTH, N_TILES, 3, D_MODEL)

    seq_lens = (SEQ, DEC_SEQ)
    tabs = {L: _dft_tables(2 * L // DFT_B) for L in seq_lens}
    feats = {L: _filter_features(L) for L in seq_lens}
    absd = _abs_deltas()
    cos_t, sin_t = _rope_tables()
    bd = _group_sum_matrix()
    w1p = jnp.pad(hy_w1, ((0, 0), (0, _FEAT_PAD - FILTER_EMB), (0, 0)))

    ks, vs = [], []
    for l in range(DEPTH):
        (pa, sa, v, x1, x2, sb, pc, sc, q, kv, sd, gm) = _inproj_call(
            x, mod_tiles[l], norm_g[l], w_in[l].astype(bf16), conv_dw_w[l], conv_dw_b[l], conv_ln_g[l],
            conv_ln_b[l], conv_pw[l].astype(bf16), hy_short_w[l], hy_short_b[l], pool_w[l].astype(bf16),
            pool_scale[l])

        spec = {}
        for L in seq_lens:
            hfilt, ss = _filter_mlp_call(L, feats[L][0], feats[L][1], w1p[l], hy_b1[l], hy_freq[l], hy_w2[l],
                                         hy_b2[l], hy_w3[l], hy_b3[l], absd)
            spec[L] = _filter_spectrum(2 * L // DFT_B, tabs[L][0], hfilt, ss)
        pb_ctx = _hy_ctx_call(tabs[SEQ], spec[SEQ], v, x1, x2, hy_skip[l])
        pb_lat = _hy_lat_call(tabs[DEC_SEQ], spec[DEC_SEQ], v, x1, x2, hy_skip[l]).reshape(N_LAT_TOK, W_BR)

        qt, kn, khm, vthm = _qk_prep_call(q, kv, q_norm[l], k_norm[l], cos_t, sin_t, bd)
        att_ctx = _attn_call(qt, khm, vthm, BATCH, SEQ, N_Q_HEADS, SEQ, 0)
        cache = (cache_k[:, l].astype(bf16).transpose(0, 2, 1, 3), cache_v[:, l].astype(bf16).transpose(0, 2, 3, 1))
        att_lat = _attn_call(qt, khm, vthm, DEC_BATCH, DEC_SEQ, Q_PER_KV, 256, N_CTX_TOK, cache)

        x = _merge_call(x, mod_tiles[l], pa, pb_ctx, pb_lat, pc, att_ctx, att_lat, sa, sb, sc, sd, gm,
                        wo_conv[l].astype(bf16), wo_hyena[l].astype(bf16), wo_pool[l].astype(bf16),
                        wo_attn[l].astype(bf16), w_out[l].astype(bf16), split_out=l == DEPTH - 1)

        ks.append(kn[:N_CTX_TOK].reshape(BATCH, SEQ, N_KV_HEADS, HEAD_DIM))
        vs.append(kv[:N_CTX_TOK, W_KV:].reshape(BATCH, SEQ, N_KV_HEADS, HEAD_DIM))

    y_prompt = x[0].reshape(BATCH, SEQ, D_MODEL)
    y_sample = x[1].reshape(DEC_BATCH, DEC_SEQ, D_MODEL)
    return (y_prompt, y_sample, jnp.stack(ks, axis=1), jnp.stack(vs, axis=1))
```

```python
import functools
import math

import numpy as np
import jax
import jax.numpy as jnp
from jax import lax
from jax.experimental import pallas as pl
from jax.experimental.pallas import tpu as pltpu

f32 = jnp.float32
bf16 = jnp.bfloat16

D_MODEL = 1024
BATCH = 16
SEQ = 256
DEPTH = 2
DEC_BATCH = 2
DEC_SEQ = 4096
PAST_LEN = 256
GRID_W = 64
W_BR = 512
CONV_K = 31
SHORT_K = 3
FILTER_BANDS = 16
FILTER_EMB = 1 + 2 * FILTER_BANDS
FILTER_HIDDEN = 64
DECAY_TARGET = 1e-2
FAST_DECAY_PCT = 0.3
SLOW_DECAY_PCT = 1.5
POOL_WINDOWS = (2, 4, 8, 16)
POOL_GROUP = W_BR // len(POOL_WINDOWS)
HEAD_DIM = 64
N_Q_HEADS = 16
N_KV_HEADS = 4
Q_PER_KV = N_Q_HEADS // N_KV_HEADS
W_ATTN = N_Q_HEADS * HEAD_DIM
W_KV = N_KV_HEADS * HEAD_DIM
ROPE_THETA = 10000.0
EPS = 1e-6
N_IN = 2 * W_BR + W_BR + 3 * W_BR + W_BR + W_BR + W_BR + W_ATTN + 2 * W_KV + W_ATTN + 4 * D_MODEL

TM = 256
N_CTX_TOK = BATCH * SEQ
N_LAT_TOK = DEC_BATCH * DEC_SEQ
N_TOK = N_CTX_TOK + N_LAT_TOK
N_TILES = N_TOK // TM
CTX_TILES = N_CTX_TOK // TM
LAT_TPS = DEC_SEQ // TM
DFT_B = 256
LANES = 128
SUBLANES = 8
VMEM_LIMIT = 56 * 1024 * 1024

assert SEQ == TM == DFT_B and CTX_TILES % LAT_TPS == 0


def _sigmoid(x):
    return 1.0 / (1.0 + jnp.exp(-x))


def _silu(x):
    return x * _sigmoid(x)


def _mm(a, b):
    return jnp.dot(a, b, preferred_element_type=f32)


def _split_bf16(a):
    hi = a.astype(bf16)
    lo = (a - hi.astype(f32)).astype(bf16)
    return hi, lo


def _mm3(a, b):
    ah, al = _split_bf16(a)
    bh, bl = _split_bf16(b)
    return _mm(ah, bh) + (_mm(ah, bl) + _mm(al, bh))


def _params(n_axes, vmem=None):
    return pltpu.CompilerParams(dimension_semantics=("arbitrary",) * n_axes,
                                vmem_limit_bytes=vmem)


def _tile_position():
    i = pl.program_id(0)
    return i >= CTX_TILES, lax.rem(i, LAT_TPS)


def _halo_flags():
    is_lat, jt = _tile_position()
    return jnp.logical_and(is_lat, jt > 0), jnp.logical_and(is_lat, jt < LAT_TPS - 1)


def _ada_kernel(c_ref, w_ref, b_ref, o_ref):
    c = c_ref[...]
    o_ref[...] = _mm(_silu(c).astype(bf16), w_ref[...].astype(bf16)) + b_ref[...]


def _ada_call(cond8, w_ada, b_ada):
    nb = 3 * D_MODEL // D_MODEL
    return pl.pallas_call(
        _ada_kernel,
        grid=(DEPTH, nb),
        in_specs=[
            pl.BlockSpec((8, D_MODEL), lambda l, j: (0, 0)),
            pl.BlockSpec((None, D_MODEL, D_MODEL), lambda l, j: (l, 0, j)),
            pl.BlockSpec((None, 1, D_MODEL), lambda l, j: (l, 0, j)),
        ],
        out_specs=pl.BlockSpec((None, 8, D_MODEL), lambda l, j: (l, 0, j)),
        out_shape=jax.ShapeDtypeStruct((DEPTH, 8, 3 * D_MODEL), f32),
        compiler_params=_params(2, VMEM_LIMIT),
        name="ada_mod",
    )(cond8, w_ada, b_ada.reshape(DEPTH, 1, 3 * D_MODEL))


_C_GLU = 0
_C_AGATE = 2 * W_BR
_C_BPROJ = _C_AGATE + W_BR
_C_BGATE = _C_BPROJ + 3 * W_BR
_C_CIN = _C_BGATE + W_BR
_C_CGATE = _C_CIN + W_BR
_C_Q = _C_CGATE + W_BR
_C_KV = _C_Q + W_ATTN
_C_DGATE = _C_KV + 2 * W_KV
_C_GM = _C_DGATE + W_ATTN
assert _C_GM + 4 * D_MODEL == N_IN


_CONF_HB = 16
_CONF_CHUNK = 32
_CONF_SPAN = TM + 2 * _CONF_HB - SUBLANES
assert (CONV_K - 1 + _CONF_HB - CONV_K // 2) // SUBLANES * SUBLANES + TM <= _CONF_SPAN


def _conformer_tasks(pad_ref, shift_ref, conv_ref, dw_ref, dwb_ref, lng_ref, lnb_ref, pw_ref, o_ref):
    def shifted_copies():
        for r in range(SUBLANES):
            shift_ref[r] = pad_ref[r:r + _CONF_SPAN, :]

    def conv_chunk(c):
        r0 = c * _CONF_CHUNK
        off = _CONF_HB - CONV_K // 2
        groups = _CONF_CHUNK // SUBLANES
        acc = jnp.broadcast_to(dwb_ref[...][None], (groups, SUBLANES, W_BR))
        for j in range(CONV_K):
            a0 = r0 + (j + off) // SUBLANES * SUBLANES
            x = shift_ref[(j + off) % SUBLANES, a0:a0 + _CONF_CHUNK, :]
            acc = acc + dw_ref[j][None] * x.reshape(groups, SUBLANES, W_BR)
        conv_ref[r0:r0 + _CONF_CHUNK, :] = acc.reshape(_CONF_CHUNK, W_BR)

    def finish():
        a = conv_ref[...]
        mu = jnp.mean(a, axis=-1, keepdims=True)
        xc = a - mu
        var = jnp.mean(xc * xc, axis=-1, keepdims=True)
        y = xc * lax.rsqrt(var + EPS) * lng_ref[...] + lnb_ref[...]
        o_ref[...] = _mm(_silu(y).astype(bf16), pw_ref[...]).astype(bf16)

    return ([shifted_copies] + [functools.partial(conv_chunk, c) for c in range(TM // _CONF_CHUNK)]
            + [finish])


def _short_conv_tasks(pad_ref, w_ref, b_ref, outs):
    def group(g):
        cs = slice(g * W_BR, (g + 1) * W_BR)
        acc = jnp.broadcast_to(b_ref[:, cs], (TM, W_BR))
        for j in range(SHORT_K):
            r0 = _CONF_HB + j - SHORT_K // 2
            acc = acc + w_ref[j:j + 1, cs] * pad_ref[r0:r0 + TM, cs]
        outs[g][...] = acc

    return [functools.partial(group, g) for g in range(len(outs))]


def _pool_tasks(pad_ref, pw_ref, ps_ref, o_ref):
    def group(g):
        is_lat, jt = _tile_position()
        seq_len = jnp.where(is_lat, DEC_SEQ, SEQ)
        t = jnp.where(is_lat, jt * TM, 0) + lax.broadcasted_iota(jnp.int32, (TM, POOL_GROUP), 0)
        cs = slice(g * POOL_GROUP, (g + 1) * POOL_GROUP)
        hw = POOL_WINDOWS[g] // 2
        s = pad_ref[_CONF_HB - hw:_CONF_HB - hw + TM, cs]
        for d in range(-hw + 1, hw):
            s = s + pad_ref[_CONF_HB + d:_CONF_HB + d + TM, cs]
        cnt = (jnp.minimum(t + hw, seq_len) - jnp.maximum(t - hw, 0)).astype(f32)
        pooled = s / cnt - pad_ref[_CONF_HB:_CONF_HB + TM, cs]
        y = _mm(pooled.astype(bf16), pw_ref[g])
        o_ref[:, cs] = (y * ps_ref[:, cs]).astype(bf16)

    return [functools.partial(group, g) for g in range(len(POOL_WINDOWS))]


def _inproj_kernel(xp_ref, xc_ref, xl_ref, xn_ref, mod_ref, g_ref, w_ref,
                   dw_ref, dwb_ref, lng_ref, lnb_ref, cpw_ref, sw_ref, sbias_ref, ppw_ref, pps_ref,
                   pa_ref, sa_ref, v_ref, x1_ref, x2_ref, sb_ref, pc_ref, sc_ref, q_ref, kv_ref, sd_ref, gm_ref,
                   pad_a, shift_a, conv_a, pad_b, pad_c):
    has_prev, has_next = _halo_flags()
    gain = g_ref[...]
    scale1 = 1.0 + mod_ref[1:2, :]
    shift = mod_ref[0:1, :]

    def modulated(x):
        ms = jnp.mean(x * x, axis=-1, keepdims=True)
        return (x * lax.rsqrt(ms + EPS) * gain * scale1 + shift).astype(bf16)

    h = modulated(jnp.where(pl.program_id(0) < CTX_TILES, xc_ref[...], xl_ref[...]))
    h_ext = jnp.concatenate([modulated(xp_ref[...]), h, modulated(xn_ref[...])], axis=0)
    rows = TM + 2 * _CONF_HB
    r = lax.broadcasted_iota(jnp.int32, (rows, 1), 0)
    valid = jnp.logical_and(jnp.logical_or(r >= _CONF_HB, has_prev),
                            jnp.logical_or(r < _CONF_HB + TM, has_next))

    def proj(c0, width):
        return _mm(h, w_ref[:, c0:c0 + width])

    def proj_ext(c0, width):
        return _mm(h_ext, w_ref[:, c0:c0 + width])

    def ext_into(dst, c0, col):
        def task():
            dst[:, col * W_BR:(col + 1) * W_BR] = jnp.where(valid, proj_ext(c0, W_BR), 0.0)
        return task

    def act_into(dst, c0, col, act, width=W_BR):
        def task():
            y = proj(c0, width)
            dst[:, col * W_BR:col * W_BR + width] = (y if act is None else act(y)).astype(dst.dtype)
        return task

    pad_a[...] = jnp.where(valid, proj_ext(_C_GLU, W_BR) * _sigmoid(proj_ext(_C_GLU + W_BR, W_BR)), 0.0)

    mxu_tasks = [ext_into(pad_b, _C_BPROJ + j * W_BR, j) for j in range(3)] + [ext_into(pad_c, _C_CIN, 0)]
    mxu_tasks += [act_into(sa_ref, _C_AGATE, 0, _silu), act_into(sb_ref, _C_BGATE, 0, _silu),
                  act_into(sc_ref, _C_CGATE, 0, _silu)]
    mxu_tasks += [act_into(q_ref, _C_Q + j * W_BR, j, None) for j in range(W_ATTN // W_BR)]
    mxu_tasks += [act_into(kv_ref, _C_KV, 0, None, 2 * W_KV)]
    mxu_tasks += [act_into(sd_ref, _C_DGATE + j * W_BR, j, _silu) for j in range(W_ATTN // W_BR)]
    mxu_tasks += [act_into(gm_ref, _C_GM + j * W_BR, j, _sigmoid) for j in range(4 * D_MODEL // W_BR)]

    vpu_tasks = _conformer_tasks(pad_a, shift_a, conv_a, dw_ref, dwb_ref, lng_ref, lnb_ref, cpw_ref, pa_ref)
    n_ext = 4
    later = (_short_conv_tasks(pad_b, sw_ref, sbias_ref, (v_ref, x1_ref, x2_ref))
             + _pool_tasks(pad_c, ppw_ref, pps_ref, pc_ref))
    vpu_tasks = vpu_tasks[:n_ext + 1] + later + vpu_tasks[n_ext + 1:]
    for k in range(max(len(mxu_tasks), len(vpu_tasks))):
        if k < len(mxu_tasks):
            mxu_tasks[k]()
        if k < len(vpu_tasks):
            vpu_tasks[k]()


def _inproj_call(x_ctx, x_lat, layer, mod_tiles_l, norm_g_l, w_in_bf, dw_w, dw_b, ln_g, ln_b, conv_pw_bf,
                 short_w, short_b, pool_w_bf, pool_scale):
    widths = [(W_BR, bf16), (W_BR, bf16), (W_BR, f32), (W_BR, f32), (W_BR, f32), (W_BR, bf16),
              (W_BR, bf16), (W_BR, bf16), (W_ATTN, f32), (2 * W_KV, f32), (W_ATTN, bf16), (4 * D_MODEL, bf16)]
    row = lambda v: v.reshape(1, -1)
    rep = lambda v: jnp.broadcast_to(v[..., None, :], v.shape[:-1] + (SUBLANES, W_BR))
    const = lambda shape: pl.BlockSpec(shape, lambda i: (0,) * len(shape))
    ext_rows = TM + 2 * _CONF_HB
    r = TM // _CONF_HB
    last_halo = N_LAT_TOK // _CONF_HB - 1
    lat_tile = lambda i: jnp.maximum(i - CTX_TILES, 0)
    return pl.pallas_call(
        _inproj_kernel,
        grid=(N_TILES,),
        in_specs=[
            pl.BlockSpec((_CONF_HB, D_MODEL), lambda i: (jnp.maximum(lat_tile(i) * r - 1, 0), 0)),
            pl.BlockSpec((TM, D_MODEL), lambda i: (jnp.minimum(i, CTX_TILES - 1), 0)),
            pl.BlockSpec((TM, D_MODEL), lambda i: (lat_tile(i), 0)),
            pl.BlockSpec((_CONF_HB, D_MODEL), lambda i: (jnp.minimum((lat_tile(i) + 1) * r, last_halo), 0)),
            pl.BlockSpec((None, 3, D_MODEL), lambda i: (i, 0, 0)),
            const((1, D_MODEL)),
            pl.BlockSpec((None, D_MODEL, N_IN), lambda i: (layer, 0, 0), pipeline_mode=pl.Buffered(1)),
            const((CONV_K, SUBLANES, W_BR)), const((SUBLANES, W_BR)), const((1, W_BR)), const((1, W_BR)),
            const((W_BR, W_BR)), const((SHORT_K, 3 * W_BR)), const((1, 3 * W_BR)),
            const((len(POOL_WINDOWS), POOL_GROUP, POOL_GROUP)), const((1, W_BR))],
        out_specs=[pl.BlockSpec((TM, w), lambda i: (i, 0)) for w, _ in widths],
        out_shape=[jax.ShapeDtypeStruct((N_TOK, w), dt) for w, dt in widths],
        scratch_shapes=[pltpu.VMEM((ext_rows, W_BR), f32), pltpu.VMEM((SUBLANES, _CONF_SPAN, W_BR), f32),
                        pltpu.VMEM((TM, W_BR), f32), pltpu.VMEM((ext_rows, 3 * W_BR), f32),
                        pltpu.VMEM((ext_rows, W_BR), f32)],
        compiler_params=_params(1, VMEM_LIMIT),
        name="in_proj",
    )(x_lat, x_ctx, x_lat, x_lat, mod_tiles_l, row(norm_g_l), w_in_bf, rep(dw_w), rep(dw_b), row(ln_g), row(ln_b), conv_pw_bf,
      short_w, row(short_b), pool_w_bf, row(pool_scale))


def _dft_tables(n2):
    n = DFT_B * n2
    k1 = np.arange(DFT_B, dtype=np.int64)[:, None]
    t1 = np.arange(DFT_B, dtype=np.int64)[None, :]
    fwd, inv = [], []
    for k2 in range(n2):
        ang = -2.0 * np.pi * (((n2 * k1 + k2) * t1) % n).astype(np.float64) / n
        gr, gi = np.cos(ang), np.sin(ang)
        fwd.append(np.concatenate([gr, gi], axis=0))
        inv.append(np.concatenate([gr.T, gi.T], axis=1))
    return (jnp.asarray(np.stack(fwd).astype(np.float32)).astype(bf16),
            jnp.asarray(np.stack(inv).astype(np.float32)).astype(bf16))


def _cmul_const(x, ang):
    c, s = math.cos(ang), math.sin(ang)
    xr, xi = x
    return (c * xr - s * xi, s * xr + c * xi)


def _fft_list(xs, sign):
    n = len(xs)
    if n == 1:
        return list(xs)
    ev = _fft_list(xs[0::2], sign)
    od = _fft_list(xs[1::2], sign)
    out = [None] * n
    for k in range(n // 2):
        orr, oi = od[k]
        if k == 0:
            tr, ti = orr, oi
        elif 4 * k == n:
            tr, ti = (-oi, orr) if sign > 0 else (oi, -orr)
        else:
            tr, ti = _cmul_const(od[k], sign * 2.0 * math.pi * k / n)
        er, ei = ev[k]
        out[k] = (er + tr, ei + ti)
        out[k + n // 2] = (er - tr, ei - ti)
    return out


def _fft_padded(xs, sign):
    m = len(xs)
    ev = _fft_list(xs, sign)
    od = _fft_list([xs[0]] + [_cmul_const(xs[t], sign * math.pi * t / m) for t in range(1, m)], sign)
    out = []
    for j in range(m):
        out += [ev[j], od[j]]
    return out


def _ifft_truncated(ws):
    m = len(ws) // 2
    ev = _fft_list(ws[0::2], 1)
    od = _fft_list(ws[1::2], 1)
    out = []
    for t in range(m):
        tr, ti = od[t] if t == 0 else _cmul_const(od[t], math.pi * t / m)
        out.append((ev[t][0] + tr, ev[t][1] + ti))
    return out


def _spectral_block(u, m, mt, hr, hi):
    w = hr.shape[1]
    hr, hi = hr.astype(f32), hi.astype(f32)
    p = _mm(m, u)
    xr = p[:DFT_B, :w] - p[DFT_B:, w:]
    xi = p[:DFT_B, w:] + p[DFT_B:, :w]
    yr = xr * hr - xi * hi
    yi = xr * hi + xi * hr
    rhs = jnp.concatenate([jnp.concatenate([yr, yi], axis=1),
                           jnp.concatenate([yi, -yr], axis=1)], axis=0).astype(bf16)
    return _mm(mt, rhs)


def _slab(s):
    return pl.ds(pl.multiple_of(s * SUBLANES, SUBLANES), SUBLANES)


_FEAT_PAD = 64


def _filter_features(seq_len):
    m = np.arange(2 * seq_len)
    j = np.where(m < seq_len, m, 2 * seq_len - m).astype(np.float64)
    t = j / (seq_len - 1)
    bands = np.linspace(1e-4, FILTER_BANDS - 1, FILTER_BANDS)[None, :]
    w = (2.0 * math.pi / seq_len) * j[:, None]
    z = np.concatenate([t[:, None], np.cos(bands * w), np.sin(bands * w)], axis=-1)
    z = np.pad(z, ((0, 0), (0, _FEAT_PAD - FILTER_EMB)))
    return jnp.asarray(z.astype(np.float32)), jnp.asarray(t.astype(np.float32)[:, None])


def _abs_deltas():
    max_decay = math.log(DECAY_TARGET) / FAST_DECAY_PCT
    min_decay = math.log(DECAY_TARGET) / SLOW_DECAY_PCT
    d = np.abs(np.linspace(min_decay, max_decay, W_BR))
    return jnp.asarray(np.concatenate([d, d]).astype(np.float32)[None, :])


def _filter_mlp_kernel(seq_len, tr, z_ref, t_ref, w1_ref, b1_ref, fr_ref, w2_ref, b2_ref, w3_ref, b3_ref,
                       ad_ref, h_ref, ss_ref):
    i = pl.program_id(0)
    fr = fr_ref[...]
    hdn = jnp.sin(fr * (_mm3(z_ref[...], w1_ref[...]) + b1_ref[...]))
    hdn = jnp.sin(fr * (_mm3(hdn, w2_ref[...]) + b2_ref[...]))
    h = _mm3(hdn, w3_ref[...]) + b3_ref[...]
    h = h * jnp.exp(-t_ref[...] * ad_ref[...])
    m = i * tr + lax.broadcasted_iota(jnp.int32, h.shape, 0)
    h = jnp.where(m == seq_len, 0.0, h)
    h_ref[...] = h

    @pl.when(i == 0)
    def _():
        ss_ref[...] = jnp.zeros_like(ss_ref)

    ss_ref[...] += jnp.sum(h * h, axis=0, keepdims=True)


def _filter_mlp_call(seq_len, feats, tcol, w1p, b1, freq, w2, b2, w3, b3, absd):
    tr = min(512, seq_len)
    steps = 2 * seq_len // tr
    half_steps = seq_len // tr
    wide = 2 * W_BR
    const = lambda shape: pl.BlockSpec(shape, lambda i: (0, 0))
    row = lambda v: v.reshape(1, -1)
    return pl.pallas_call(
        functools.partial(_filter_mlp_kernel, seq_len, tr),
        grid=(steps,),
        in_specs=[
            pl.BlockSpec((tr, _FEAT_PAD), lambda i: (i, 0)),
            pl.BlockSpec((tr, 1), lambda i: (i, 0)),
            const((_FEAT_PAD, FILTER_HIDDEN)), const((1, FILTER_HIDDEN)), const((1, FILTER_HIDDEN)),
            const((FILTER_HIDDEN, FILTER_HIDDEN)), const((1, FILTER_HIDDEN)),
            pl.BlockSpec((FILTER_HIDDEN, wide), lambda i: (0, i // half_steps)),
            pl.BlockSpec((1, wide), lambda i: (0, i // half_steps)),
            const((1, wide)),
        ],
        out_specs=[pl.BlockSpec((tr, wide), lambda i: (i, 0)), const((1, wide))],
        out_shape=[jax.ShapeDtypeStruct((2 * seq_len, wide), f32), jax.ShapeDtypeStruct((1, wide), f32)],
        compiler_params=_params(1),
        name="filter_mlp",
    )(feats, tcol, w1p, row(b1), row(freq), w2, row(b2), w3, row(b3), absd)


_RADIX_ROWS = 128
_K2_PER_STEP = 2


def _filter_radix_kernel(n2, h_ref, ur_ref, ui_ref, sr_ref, si_ref):
    zero = jnp.zeros((SUBLANES, LANES), f32)

    def body(s, carry):
        rs = _slab(s)
        us = _fft_list([(h_ref[t, rs, :], zero) for t in range(n2)], -1)
        for k in range(n2):
            sr_ref[k, rs, :] = us[k][0]
            si_ref[k, rs, :] = us[k][1]
        return carry

    lax.fori_loop(0, _RADIX_ROWS // SUBLANES, body, 0)
    for k in range(n2):
        ur_ref[k] = sr_ref[k].astype(bf16)
        ui_ref[k] = si_ref[k].astype(bf16)


def _filter_spec_kernel(scale_const, ur_ref, ui_ref, m_ref, ss_ref, hr_ref, hi_ref):
    w = ur_ref.shape[2]
    scale = lax.rsqrt(ss_ref[...] + EPS) * scale_const
    for k in range(ur_ref.shape[0]):
        p = _mm(m_ref[k], jnp.concatenate([ur_ref[k], ui_ref[k]], axis=1))
        hr_ref[k] = ((p[:DFT_B, :w] - p[DFT_B:, w:]) * scale).astype(hr_ref.dtype)
        hi_ref[k] = ((p[:DFT_B, w:] + p[DFT_B:, :w]) * scale).astype(hi_ref.dtype)


def _filter_spectrum(n2, mfwd, hfilt, ss):
    wide = 2 * W_BR
    blk = pl.BlockSpec((n2, _RADIX_ROWS, LANES), lambda r, c: (0, r, c))
    u_shape = jax.ShapeDtypeStruct((n2, DFT_B, wide), bf16)
    ur, ui = pl.pallas_call(
        functools.partial(_filter_radix_kernel, n2),
        grid=(DFT_B // _RADIX_ROWS, wide // LANES),
        in_specs=[blk],
        out_specs=[blk, blk],
        out_shape=[u_shape, u_shape],
        scratch_shapes=[pltpu.VMEM((n2, _RADIX_ROWS, LANES), f32)] * 2,
        compiler_params=_params(2),
        name="filter_radix",
    )(hfilt.reshape(n2, DFT_B, wide))
    kspec = pl.BlockSpec((_K2_PER_STEP, DFT_B, wide), lambda k: (k, 0, 0))
    h_shape = jax.ShapeDtypeStruct((n2, DFT_B, wide), bf16)
    return pl.pallas_call(
        functools.partial(_filter_spec_kernel, 1.0 / (n2 * DFT_B)),
        grid=(n2 // _K2_PER_STEP,),
        in_specs=[kspec, kspec, pl.BlockSpec((_K2_PER_STEP, 2 * DFT_B, DFT_B), lambda k: (k, 0, 0)),
                  pl.BlockSpec((1, wide), lambda k: (0, 0))],
        out_specs=[kspec, kspec],
        out_shape=[h_shape, h_shape],
        compiler_params=_params(1),
        name="filter_spec",
    )(ur, ui, mfwd, ss)


def _hy_ctx_kernel(va_ref, vb_ref, x1a_ref, x1b_ref, x2a_ref, x2b_ref, m_ref, mt_ref, hr_ref, hi_ref,
                   skip_ref, o_ref):
    n2 = m_ref.shape[0]
    za, zb = va_ref[...], vb_ref[...]
    gates = ((x1a_ref, x1b_ref), (x2a_ref, x2b_ref))
    for order in range(2):
        cs = slice(order * W_BR, (order + 1) * W_BR)
        u = jnp.concatenate([za, zb], axis=1).astype(bf16)
        w = _spectral_block(u, m_ref[0], mt_ref[0], hr_ref[0, :, cs], hi_ref[0, :, cs])
        for k2 in range(1, n2):
            w = w + _spectral_block(u, m_ref[k2], mt_ref[k2], hr_ref[k2, :, cs], hi_ref[k2, :, cs])
        skip = skip_ref[order:order + 1, :]
        za = gates[order][0][...] * (w[:, :W_BR] + za * skip)
        zb = gates[order][1][...] * (w[:, W_BR:] + zb * skip)
    o_ref[0:DFT_B, :] = za
    o_ref[DFT_B:, :] = zb


def _hy_ctx_call(tabs, spec, v, x1, x2, skip):
    mfwd, minv = tabs
    n2 = mfwd.shape[0]
    assert n2 == 2
    seq = lambda par: pl.BlockSpec((DFT_B, W_BR), lambda p: (2 * p + par, 0))
    whole = lambda a: pl.BlockSpec(a.shape, lambda p: (0,) * a.ndim)
    return pl.pallas_call(
        _hy_ctx_kernel,
        grid=(BATCH // 2,),
        in_specs=[seq(0), seq(1), seq(0), seq(1), seq(0), seq(1), whole(mfwd), whole(minv),
                  whole(spec[0]), whole(spec[1]), whole(skip)],
        out_specs=pl.BlockSpec((2 * DFT_B, W_BR), lambda p: (p, 0)),
        out_shape=jax.ShapeDtypeStruct((N_CTX_TOK, W_BR), f32),
        compiler_params=_params(1, VMEM_LIMIT),
        name="hyena_ctx",
    )(v, v, x1, x1, x2, x2, mfwd, minv, spec[0], spec[1], skip)


def _store_radix(us, rs, sr_ref, si_ref):
    for k, (ur, ui) in enumerate(us):
        sr_ref[k, rs, :] = ur
        si_ref[k, rs, :] = ui


def _cast_radix(sr_ref, si_ref, ur_ref, ui_ref):
    for k in range(sr_ref.shape[0]):
        ur_ref[k] = sr_ref[k].astype(bf16)
        ui_ref[k] = si_ref[k].astype(bf16)


def _hy_fwd_radix_kernel(m, za_ref, zb_ref, ur_ref, ui_ref, sr_ref, si_ref):
    def body(s, carry):
        rs = _slab(s)
        _store_radix(_fft_padded([(za_ref[t, rs, :], zb_ref[t, rs, :]) for t in range(m)], -1),
                     rs, sr_ref, si_ref)
        return carry

    lax.fori_loop(0, _RADIX_ROWS // SUBLANES, body, 0)
    _cast_radix(sr_ref, si_ref, ur_ref, ui_ref)


def _hy_spec_kernel(ur_ref, ui_ref, m_ref, mt_ref, hr_ref, hi_ref, w_ref):
    for k in range(ur_ref.shape[0]):
        u = jnp.concatenate([ur_ref[k], ui_ref[k]], axis=1)
        w_ref[k] = _spectral_block(u, m_ref[k], mt_ref[k], hr_ref[k], hi_ref[k]).astype(w_ref.dtype)


def _hy_inv_radix_kernel(m, final, wr_ref, wi_ref, za_ref, zb_ref, ga_ref, gb_ref, skip_ref, o_ref, *rest):
    skip = skip_ref[...]
    sr_ref, si_ref = rest[-2:]
    for k in range(2 * m):
        sr_ref[k] = wr_ref[k].astype(f32)
        si_ref[k] = wi_ref[k].astype(f32)

    def body(s, carry):
        rs = _slab(s)
        ys = _ifft_truncated([(sr_ref[k, rs, :], si_ref[k, rs, :]) for k in range(2 * m)])
        zs = []
        for t in range(m):
            za = ga_ref[t, rs, :] * (ys[t][0] + za_ref[t, rs, :] * skip)
            zb = gb_ref[t, rs, :] * (ys[t][1] + zb_ref[t, rs, :] * skip)
            o_ref[0, t, rs, :] = za
            o_ref[1, t, rs, :] = zb
            zs.append((za, zb))
        if not final:
            _store_radix(_fft_padded(zs, -1), rs, sr_ref, si_ref)
        return carry

    lax.fori_loop(0, _RADIX_ROWS // SUBLANES, body, 0)
    if not final:
        _cast_radix(sr_ref, si_ref, rest[0], rest[1])


def _hy_lat_call(tabs, spec, v, x1, x2, skip):
    mfwd, minv = tabs
    n2 = mfwd.shape[0]
    m = n2 // 2
    assert m * DFT_B == DEC_SEQ and DEC_BATCH == 2
    seq0 = N_CTX_TOK // DEC_SEQ
    grid = (DFT_B // _RADIX_ROWS, W_BR // LANES)

    def tok(arr, b):
        return (arr.reshape(N_TOK // DEC_SEQ, m, DFT_B, W_BR),
                pl.BlockSpec((None, m, _RADIX_ROWS, LANES), lambda r, c: (seq0 + b, 0, r, c)))

    def pair(arr, b):
        return (arr, pl.BlockSpec((None, m, _RADIX_ROWS, LANES), lambda r, c: (b, 0, r, c)))

    u_spec = pl.BlockSpec((n2, _RADIX_ROWS, LANES), lambda r, c: (0, r, c))
    u_shape = jax.ShapeDtypeStruct((n2, DFT_B, W_BR), bf16)
    z_spec = pl.BlockSpec((2, m, _RADIX_ROWS, LANES), lambda r, c: (0, 0, r, c))
    z_shape = jax.ShapeDtypeStruct((2, m, DFT_B, W_BR), f32)
    scratch = [pltpu.VMEM((n2, _RADIX_ROWS, LANES), f32)] * 2

    def spectral(ur, ui, order):
        kb = _K2_PER_STEP
        kspec = pl.BlockSpec((kb, DFT_B, W_BR), lambda k: (k, 0, 0))
        hspec = pl.BlockSpec((kb, DFT_B, W_BR), lambda k: (k, 0, order))
        return pl.pallas_call(
            _hy_spec_kernel,
            grid=(n2 // kb,),
            in_specs=[kspec, kspec, pl.BlockSpec((kb, 2 * DFT_B, DFT_B), lambda k: (k, 0, 0)),
                      pl.BlockSpec((kb, DFT_B, 2 * DFT_B), lambda k: (k, 0, 0)), hspec, hspec],
            out_specs=pl.BlockSpec((kb, DFT_B, 2 * W_BR), lambda k: (k, 0, 0)),
            out_shape=jax.ShapeDtypeStruct((n2, DFT_B, 2 * W_BR), bf16),
            compiler_params=_params(1),
            name="hyena_spec",
        )(ur, ui, mfwd, minv, spec[0], spec[1])

    def inverse(w, z_srcs, gate, order, final):
        wr_spec = pl.BlockSpec((n2, _RADIX_ROWS, LANES), lambda r, c: (0, r, c))
        wi_spec = pl.BlockSpec((n2, _RADIX_ROWS, LANES), lambda r, c: (0, r, W_BR // LANES + c))
        g_srcs = [tok(gate, 0), tok(gate, 1)]
        return pl.pallas_call(
            functools.partial(_hy_inv_radix_kernel, m, final),
            grid=grid,
            in_specs=[wr_spec, wi_spec, z_srcs[0][1], z_srcs[1][1], g_srcs[0][1], g_srcs[1][1],
                      pl.BlockSpec((1, LANES), lambda r, c: (0, c))],
            out_specs=z_spec if final else [z_spec, u_spec, u_spec],
            out_shape=z_shape if final else [z_shape, u_shape, u_shape],
            scratch_shapes=scratch,
            compiler_params=_params(2, VMEM_LIMIT),
            name="hyena_inv_radix",
        )(w, w, z_srcs[0][0], z_srcs[1][0], g_srcs[0][0], g_srcs[1][0], skip[order].reshape(1, W_BR))

    v_srcs = [tok(v, 0), tok(v, 1)]
    ur, ui = pl.pallas_call(
        functools.partial(_hy_fwd_radix_kernel, m),
        grid=grid,
        in_specs=[v_srcs[0][1], v_srcs[1][1]],
        out_specs=[u_spec, u_spec],
        out_shape=[u_shape, u_shape],
        scratch_shapes=scratch,
        compiler_params=_params(2),
        name="hyena_fwd_radix",
    )(v_srcs[0][0], v_srcs[1][0])
    z1, ur, ui = inverse(spectral(ur, ui, 0), v_srcs, x1, 0, False)
    return inverse(spectral(ur, ui, 1), [pair(z1, 0), pair(z1, 1)], x2, 1, True)


_QK_SCALE = HEAD_DIM ** -0.5 * math.log2(math.e)


def _rope_tables():
    half = HEAD_DIM // 2
    f = half // 2
    inv = ROPE_THETA ** (-np.arange(f, dtype=np.float64) / f)
    pos = np.arange(DEC_SEQ)
    row, col = (pos // GRID_W).astype(np.float64), (pos % GRID_W).astype(np.float64)
    ang = np.concatenate([np.tile(row[:, None] * inv[None, :], (1, 2)),
                          np.tile(col[:, None] * inv[None, :], (1, 2))], axis=1)
    sign = np.tile(np.concatenate([-np.ones(f), np.ones(f)]), 2)[None, :]
    cos = np.concatenate([np.ones((TM, HEAD_DIM)), np.cos(ang)], axis=0)
    sin = np.concatenate([np.zeros((TM, HEAD_DIM)), np.sin(ang) * sign], axis=0)
    rep = W_KV // HEAD_DIM
    return (jnp.asarray(np.tile(cos, (1, rep)).astype(np.float32)),
            jnp.asarray(np.tile(sin, (1, rep)).astype(np.float32)))


def _group_sum_matrix():
    idx = np.arange(W_KV) // HEAD_DIM
    return jnp.asarray((idx[:, None] == idx[None, :]).astype(np.float32)).astype(bf16)


def _qk_prep_kernel(q_ref, kv_ref, gq_ref, gk_ref, cos_ref, sin_ref, bd_ref,
                    qt_ref, kn_ref, khm_ref, vthm_ref):
    cos = cos_ref[...]
    sin = sin_ref[...]
    bd = bd_ref[...]
    eye = (lax.broadcasted_iota(jnp.int32, (HEAD_DIM, HEAD_DIM), 0)
           == lax.broadcasted_iota(jnp.int32, (HEAD_DIM, HEAD_DIM), 1)).astype(f32).astype(bf16)
    lane = lax.broadcasted_iota(jnp.int32, (TM, W_KV), 1)
    first_half = jnp.bitwise_and(lane, HEAD_DIM // 2 - 1) < HEAD_DIM // 4
    quarter = HEAD_DIM // 4

    def norm(x, g):
        hi, lo = _split_bf16(x * x)
        ss = _mm(hi, bd) + _mm(lo, bd)
        return x * lax.rsqrt(ss * (1.0 / HEAD_DIM) + EPS) * g

    def transposed(x):
        return lax.dot_general(eye, x.astype(bf16), (((1,), (1,)), ((), ())),
                               preferred_element_type=f32).astype(bf16)

    def rope(x):
        swapped = jnp.where(first_half, pltpu.roll(x, W_KV - quarter, 1), pltpu.roll(x, quarter, 1))
        return x * cos + swapped * sin

    for j in range(W_ATTN // W_KV):
        cs = slice(j * W_KV, (j + 1) * W_KV)
        qs = rope(norm(q_ref[:, cs], gq_ref[...])) * _QK_SCALE
        for hh in range(N_KV_HEADS):
            qt_ref[j * N_KV_HEADS + hh] = transposed(qs[:, hh * HEAD_DIM:(hh + 1) * HEAD_DIM])
    kn = norm(kv_ref[:, :W_KV], gk_ref[...])
    kn_ref[...] = kn
    kr = rope(kn)
    v = kv_ref[:, W_KV:]
    for g in range(N_KV_HEADS):
        hs = slice(g * HEAD_DIM, (g + 1) * HEAD_DIM)
        khm_ref[g] = kr[:, hs].astype(bf16)
        vthm_ref[g] = transposed(v[:, hs])


def _qk_prep_call(q, kv, q_norm, k_norm, cos_t, sin_t, bd):
    rep = W_KV // HEAD_DIM
    const = lambda shape: pl.BlockSpec(shape, lambda i: (0, 0))
    tab = pl.BlockSpec((TM, W_KV), lambda i: (jnp.where(i < CTX_TILES, 0, 1 + lax.rem(i, LAT_TPS)), 0))
    tile = lambda w: pl.BlockSpec((TM, w), lambda i: (i, 0))
    return pl.pallas_call(
        _qk_prep_kernel,
        grid=(N_TILES,),
        in_specs=[tile(W_ATTN), tile(2 * W_KV), const((1, W_KV)), const((1, W_KV)), tab, tab,
                  const((W_KV, W_KV))],
        out_specs=[pl.BlockSpec((N_Q_HEADS, HEAD_DIM, TM), lambda i: (0, 0, i)), tile(W_KV),
                   pl.BlockSpec((N_KV_HEADS, TM, HEAD_DIM), lambda i: (0, i, 0)),
                   pl.BlockSpec((N_KV_HEADS, HEAD_DIM, TM), lambda i: (0, 0, i))],
        out_shape=[jax.ShapeDtypeStruct((N_Q_HEADS, HEAD_DIM, N_TOK), bf16),
                   jax.ShapeDtypeStruct((N_TOK, W_KV), f32),
                   jax.ShapeDtypeStruct((N_KV_HEADS, N_TOK, HEAD_DIM), bf16),
                   jax.ShapeDtypeStruct((N_KV_HEADS, HEAD_DIM, N_TOK), bf16)],
        compiler_params=_params(1),
        name="qk_prep",
    )(q, kv, jnp.tile(q_norm, rep).reshape(1, W_KV), jnp.tile(k_norm, rep).reshape(1, W_KV),
      cos_t, sin_t, bd)


_ATT_CK = 256


def _attn_kernel(hps, has_cache, seq_len, qt_ref, k_ref, vt_ref, *rest):
    if has_cache:
        kc_ref, vct_ref, o_ref, s_ref = rest
    else:
        o_ref, s_ref = rest
    chunks = ([(True, c) for c in range(PAST_LEN // _ATT_CK)] if has_cache else [])
    chunks += [(False, c) for c in range(seq_len // _ATT_CK)]
    tq = qt_ref.shape[2]
    groups = _ATT_CK // SUBLANES

    def span(c):
        return slice(c * _ATT_CK, (c + 1) * _ATT_CK)

    def k_chunk(g, chunk):
        cached, c = chunk
        return kc_ref[g, span(c), :] if cached else k_ref[g, span(c), :]

    def vt_chunk(g, chunk):
        cached, c = chunk
        return vct_ref[g, :, span(c)] if cached else vt_ref[g, :, span(c)]

    def score_chunk(h, i, m8):
        s_ref[h % 2, span(i), :] = _mm(k_chunk(h // Q_PER_KV, chunks[i]), qt_ref[h])
        return jnp.maximum(m8, jnp.max(s_ref[h % 2, span(i), :].reshape(groups, SUBLANES, tq), axis=0))

    neg = jnp.full((SUBLANES, tq), -jnp.inf, f32)
    m8 = neg
    for i in range(len(chunks)):
        m8 = score_chunk(0, i, m8)
    for h in range(hps):
        m = jnp.max(m8, axis=0, keepdims=True)
        m8 = neg
        l8 = jnp.zeros((SUBLANES, tq), f32)
        acc = jnp.zeros((HEAD_DIM, tq), f32)
        for i in range(len(chunks)):
            if h + 1 < hps:
                m8 = score_chunk(h + 1, i, m8)
            p = jnp.exp2(s_ref[h % 2, span(i), :] - m)
            l8 = l8 + jnp.sum(p.reshape(groups, SUBLANES, tq), axis=0)
            acc = acc + _mm(vt_chunk(h // Q_PER_KV, chunks[i]), p.astype(bf16))
        l = jnp.sum(l8, axis=0, keepdims=True)
        o_ref[:, h * HEAD_DIM:(h + 1) * HEAD_DIM] = (acc / l).T.astype(bf16)


def _attn_call(qt, khm, vthm, n_seq, seq_len, hps, tq, tok0, cache=None):
    kvb = hps // Q_PER_KV
    qtiles = seq_len // tq
    row0 = tok0 // tq
    seq0 = tok0 // seq_len
    in_specs = [
        pl.BlockSpec((hps, HEAD_DIM, tq), lambda b, g, t: (g, 0, row0 + b * qtiles + t)),
        pl.BlockSpec((kvb, seq_len, HEAD_DIM), lambda b, g, t: (g, seq0 + b, 0)),
        pl.BlockSpec((kvb, HEAD_DIM, seq_len), lambda b, g, t: (g, 0, seq0 + b)),
    ]
    args = [qt, khm, vthm]
    if cache is not None:
        in_specs += [pl.BlockSpec((None, kvb, PAST_LEN, HEAD_DIM), lambda b, g, t: (b, g, 0, 0)),
                     pl.BlockSpec((None, kvb, HEAD_DIM, PAST_LEN), lambda b, g, t: (b, g, 0, 0))]
        args += list(cache)
    return pl.pallas_call(
        functools.partial(_attn_kernel, hps, cache is not None, seq_len),
        grid=(n_seq, N_Q_HEADS // hps, qtiles),
        in_specs=in_specs,
        out_specs=pl.BlockSpec((tq, hps * HEAD_DIM), lambda b, g, t: (b * qtiles + t, g)),
        out_shape=jax.ShapeDtypeStruct((n_seq * seq_len, W_ATTN), bf16),
        scratch_shapes=[pltpu.VMEM((2, seq_len + (PAST_LEN if cache is not None else 0), tq), f32)],
        compiler_params=_params(3, VMEM_LIMIT),
        name="attention",
    )(*args)


def _merge_kernel(xc_ref, xl_ref, mod_ref, pa_ref, pbc_ref, pbl_ref, pc_ref, attc_ref, attl_ref,
                  sa_ref, sb_ref, sc_ref, sd_ref, gm_ref,
                  woa_ref, wob_ref, woc_ref, wod_ref, wout_ref, oc_ref, ol_ref):
    is_ctx = pl.program_id(0) < CTX_TILES

    def gated(p, s_ref):
        return (p.astype(f32) * s_ref[...].astype(f32)).astype(bf16)

    def gm(j):
        return gm_ref[:, j * D_MODEL:(j + 1) * D_MODEL].astype(f32)

    pb = jnp.where(is_ctx, pbc_ref[...], pbl_ref[...])
    att = jnp.where(is_ctx, attc_ref[...], attl_ref[...])
    merged = gm(0) * _mm(gated(pa_ref[...], sa_ref), woa_ref[...])
    merged = merged + gm(1) * _mm(gated(pb, sb_ref), wob_ref[...])
    merged = merged + gm(2) * _mm(gated(pc_ref[...], sc_ref), woc_ref[...])
    merged = merged + gm(3) * _mm(gated(att, sd_ref), wod_ref[...])
    y = _mm(merged.astype(bf16), wout_ref[...])
    out = jnp.where(is_ctx, xc_ref[...], xl_ref[...]) + mod_ref[2:3, :] * y
    ol_ref[...] = out

    @pl.when(is_ctx)
    def _():
        oc_ref[...] = out


def _merge_call(x_ctx, x_lat, mod_tiles_l, pa, pb_ctx, pb_lat, pc, att_ctx, att_lat, sa, sb, sc, sd, gm,
                woa, wob, woc, wod, wout):
    tile = lambda w: pl.BlockSpec((TM, w), lambda i: (i, 0))
    ctx_tile = lambda w: pl.BlockSpec((TM, w), lambda i: (jnp.minimum(i, CTX_TILES - 1), 0))
    lat_tile = lambda w: pl.BlockSpec((TM, w), lambda i: (jnp.maximum(i - CTX_TILES, 0), 0))
    const = lambda shape: pl.BlockSpec(shape, lambda i: (0, 0))
    return pl.pallas_call(
        _merge_kernel,
        grid=(N_TILES,),
        in_specs=[ctx_tile(D_MODEL), lat_tile(D_MODEL), pl.BlockSpec((None, 3, D_MODEL), lambda i: (i, 0, 0)),
                  tile(W_BR), ctx_tile(W_BR), lat_tile(W_BR), tile(W_BR),
                  ctx_tile(W_ATTN), lat_tile(W_ATTN),
                  tile(W_BR), tile(W_BR), tile(W_BR), tile(W_ATTN), tile(4 * D_MODEL),
                  const((W_BR, D_MODEL)), const((W_BR, D_MODEL)), const((W_BR, D_MODEL)),
                  const((W_ATTN, D_MODEL)), const((D_MODEL, D_MODEL))],
        out_specs=[ctx_tile(D_MODEL), lat_tile(D_MODEL)],
        out_shape=[jax.ShapeDtypeStruct((N_CTX_TOK, D_MODEL), f32), jax.ShapeDtypeStruct((N_LAT_TOK, D_MODEL), f32)],
        compiler_params=_params(1, VMEM_LIMIT),
        name="merge_out",
    )(x_ctx, x_lat, mod_tiles_l, pa, pb_ctx, pb_lat, pc, att_ctx, att_lat, sa, sb, sc, sd, gm,
      woa, wob, woc, wod, wout)


def kernel(x_prompt, x_sample, cache_k, cache_v, c, c_ctx, w_ada, b_ada, norm_g, w_in, conv_dw_w, conv_dw_b, conv_ln_g, conv_ln_b, conv_pw, hy_short_w, hy_short_b, hy_w1, hy_b1, hy_freq, hy_w2, hy_b2, hy_w3, hy_b3, hy_skip, pool_w, pool_scale, q_norm, k_norm, wo_conv, wo_hyena, wo_pool, wo_attn, w_out):
    x_ctx, x_lat = x_prompt.reshape(N_CTX_TOK, D_MODEL), x_sample.reshape(N_LAT_TOK, D_MODEL)

    cond8 = jnp.concatenate([c_ctx[None, :], c, jnp.zeros((8 - 1 - DEC_BATCH, D_MODEL), f32)], axis=0)
    mod = _ada_call(cond8, w_ada, b_ada)
    tile_cond = np.concatenate([np.zeros(CTX_TILES, np.int32),
                                1 + np.arange(N_TILES - CTX_TILES, dtype=np.int32) // LAT_TPS])
    mod_tiles = mod[:, tile_cond].reshape(DEPTH, N_TILES, 3, D_MODEL)

    seq_lens = (SEQ, DEC_SEQ)
    tabs = {L: _dft_tables(2 * L // DFT_B) for L in seq_lens}
    feats = {L: _filter_features(L) for L in seq_lens}
    absd = _abs_deltas()
    cos_t, sin_t = _rope_tables()
    bd = _group_sum_matrix()
    w_in_bf = w_in.astype(bf16)
    w1p = jnp.pad(hy_w1, ((0, 0), (0, _FEAT_PAD - FILTER_EMB), (0, 0)))

    ks, vs = [], []
    for l in range(DEPTH):
        (pa, sa, v, x1, x2, sb, pc, sc, q, kv, sd, gm) = _inproj_call(
            x_ctx, x_lat, l, mod_tiles[l], norm_g[l], w_in_bf, conv_dw_w[l], conv_dw_b[l], conv_ln_g[l],
            conv_ln_b[l], conv_pw[l].astype(bf16), hy_short_w[l], hy_short_b[l], pool_w[l].astype(bf16),
            pool_scale[l])

        spec = {}
        for L in seq_lens:
            hfilt, ss = _filter_mlp_call(L, feats[L][0], feats[L][1], w1p[l], hy_b1[l], hy_freq[l], hy_w2[l],
                                         hy_b2[l], hy_w3[l], hy_b3[l], absd)
            spec[L] = _filter_spectrum(2 * L // DFT_B, tabs[L][0], hfilt, ss)
        pb_ctx = _hy_ctx_call(tabs[SEQ], spec[SEQ], v, x1, x2, hy_skip[l])
        pb_lat = _hy_lat_call(tabs[DEC_SEQ], spec[DEC_SEQ], v, x1, x2, hy_skip[l]).reshape(N_LAT_TOK, W_BR)

        qt, kn, khm, vthm = _qk_prep_call(q, kv, q_norm[l], k_norm[l], cos_t, sin_t, bd)
        att_ctx = _attn_call(qt, khm, vthm, BATCH, SEQ, N_Q_HEADS, SEQ, 0)
        cache = (cache_k[:, l].astype(bf16).transpose(0, 2, 1, 3), cache_v[:, l].astype(bf16).transpose(0, 2, 3, 1))
        att_lat = _attn_call(qt, khm, vthm, DEC_BATCH, DEC_SEQ, Q_PER_KV, 256, N_CTX_TOK, cache)

        x_ctx, x_lat = _merge_call(x_ctx, x_lat, mod_tiles[l], pa, pb_ctx, pb_lat, pc, att_ctx, att_lat, sa, sb, sc, sd, gm,
                        wo_conv[l].astype(bf16), wo_hyena[l].astype(bf16), wo_pool[l].astype(bf16),
                        wo_attn[l].astype(bf16), w_out[l].astype(bf16))

        ks.append(kn[:N_CTX_TOK].reshape(BATCH, SEQ, N_KV_HEADS, HEAD_DIM))
        vs.append(kv[:N_CTX_TOK, W_KV:].reshape(BATCH, SEQ, N_KV_HEADS, HEAD_DIM))

    y_prompt = x_ctx.reshape(BATCH, SEQ, D_MODEL)
    y_sample = x_lat.reshape(DEC_BATCH, DEC_SEQ, D_MODEL)
    return (y_prompt, y_sample, jnp.stack(ks, axis=1), jnp.stack(vs, axis=1))
```

```python
import functools
import math

import numpy as np
import jax
import jax.numpy as jnp
from jax import lax
from jax.experimental import pallas as pl
from jax.experimental.pallas import tpu as pltpu

f32 = jnp.float32
bf16 = jnp.bfloat16

D_MODEL = 1024
BATCH = 16
SEQ = 256
DEPTH = 2
DEC_BATCH = 2
DEC_SEQ = 4096
PAST_LEN = 256
GRID_W = 64
W_BR = 512
CONV_K = 31
SHORT_K = 3
FILTER_BANDS = 16
FILTER_EMB = 1 + 2 * FILTER_BANDS
FILTER_HIDDEN = 64
DECAY_TARGET = 1e-2
FAST_DECAY_PCT = 0.3
SLOW_DECAY_PCT = 1.5
POOL_WINDOWS = (2, 4, 8, 16)
POOL_GROUP = W_BR // len(POOL_WINDOWS)
HEAD_DIM = 64
N_Q_HEADS = 16
N_KV_HEADS = 4
Q_PER_KV = N_Q_HEADS // N_KV_HEADS
W_ATTN = N_Q_HEADS * HEAD_DIM
W_KV = N_KV_HEADS * HEAD_DIM
ROPE_THETA = 10000.0
EPS = 1e-6
N_IN = 2 * W_BR + W_BR + 3 * W_BR + W_BR + W_BR + W_BR + W_ATTN + 2 * W_KV + W_ATTN + 4 * D_MODEL

TM = 256
N_CTX_TOK = BATCH * SEQ
N_LAT_TOK = DEC_BATCH * DEC_SEQ
N_TOK = N_CTX_TOK + N_LAT_TOK
N_TILES = N_TOK // TM
CTX_TILES = N_CTX_TOK // TM
LAT_TPS = DEC_SEQ // TM
DFT_B = 256
LANES = 128
SUBLANES = 8
VMEM_LIMIT = 56 * 1024 * 1024

assert SEQ == TM == DFT_B and CTX_TILES % LAT_TPS == 0


def _sigmoid(x):
    return 1.0 / (1.0 + jnp.exp(-x))


def _silu(x):
    return x * _sigmoid(x)


def _mm(a, b):
    return jnp.dot(a, b, preferred_element_type=f32)


def _split_bf16(a):
    hi = a.astype(bf16)
    lo = (a - hi.astype(f32)).astype(bf16)
    return hi, lo


def _mm3(a, b):
    ah, al = _split_bf16(a)
    bh, bl = _split_bf16(b)
    return _mm(ah, bh) + (_mm(ah, bl) + _mm(al, bh))


def _params(n_axes, vmem=None):
    return pltpu.CompilerParams(dimension_semantics=("arbitrary",) * n_axes,
                                vmem_limit_bytes=vmem)


def _tile_position():
    i = pl.program_id(0)
    return i >= CTX_TILES, lax.rem(i, LAT_TPS)


def _halo_flags():
    is_lat, jt = _tile_position()
    return jnp.logical_and(is_lat, jt > 0), jnp.logical_and(is_lat, jt < LAT_TPS - 1)


def _ada_kernel(c_ref, w_ref, b_ref, o_ref):
    c = c_ref[...]
    o_ref[...] = _mm(_silu(c).astype(bf16), w_ref[...].astype(bf16)) + b_ref[...]


def _ada_call(cond8, w_ada, b_ada):
    nb = 3 * D_MODEL // D_MODEL
    return pl.pallas_call(
        _ada_kernel,
        grid=(DEPTH, nb),
        in_specs=[
            pl.BlockSpec((8, D_MODEL), lambda l, j: (0, 0)),
            pl.BlockSpec((None, D_MODEL, D_MODEL), lambda l, j: (l, 0, j)),
            pl.BlockSpec((None, 1, D_MODEL), lambda l, j: (l, 0, j)),
        ],
        out_specs=pl.BlockSpec((None, 8, D_MODEL), lambda l, j: (l, 0, j)),
        out_shape=jax.ShapeDtypeStruct((DEPTH, 8, 3 * D_MODEL), f32),
        compiler_params=_params(2, VMEM_LIMIT),
        name="ada_mod",
    )(cond8, w_ada, b_ada.reshape(DEPTH, 1, 3 * D_MODEL))


_C_GLU = 0
_C_AGATE = 2 * W_BR
_C_BPROJ = _C_AGATE + W_BR
_C_BGATE = _C_BPROJ + 3 * W_BR
_C_CIN = _C_BGATE + W_BR
_C_CGATE = _C_CIN + W_BR
_C_Q = _C_CGATE + W_BR
_C_KV = _C_Q + W_ATTN
_C_DGATE = _C_KV + 2 * W_KV
_C_GM = _C_DGATE + W_ATTN
assert _C_GM + 4 * D_MODEL == N_IN


_CONF_HB = 16
_CONF_CHUNK = 32
_CONF_SPAN = TM + 2 * _CONF_HB - SUBLANES
assert (CONV_K - 1 + _CONF_HB - CONV_K // 2) // SUBLANES * SUBLANES + TM <= _CONF_SPAN


def _conformer_tasks(pad_ref, shift_ref, conv_ref, dw_ref, dwb_ref, lng_ref, lnb_ref, pw_ref, o_ref):
    def shifted_copies():
        for r in range(SUBLANES):
            shift_ref[r] = pad_ref[r:r + _CONF_SPAN, :]

    def conv_chunk(c):
        r0 = c * _CONF_CHUNK
        off = _CONF_HB - CONV_K // 2
        groups = _CONF_CHUNK // SUBLANES
        acc = jnp.broadcast_to(dwb_ref[...][None], (groups, SUBLANES, W_BR))
        for j in range(CONV_K):
            a0 = r0 + (j + off) // SUBLANES * SUBLANES
            x = shift_ref[(j + off) % SUBLANES, a0:a0 + _CONF_CHUNK, :]
            acc = acc + dw_ref[j][None] * x.reshape(groups, SUBLANES, W_BR)
        conv_ref[r0:r0 + _CONF_CHUNK, :] = acc.reshape(_CONF_CHUNK, W_BR)

    def finish():
        a = conv_ref[...]
        mu = jnp.mean(a, axis=-1, keepdims=True)
        xc = a - mu
        var = jnp.mean(xc * xc, axis=-1, keepdims=True)
        y = xc * lax.rsqrt(var + EPS) * lng_ref[...] + lnb_ref[...]
        o_ref[...] = _mm(_silu(y).astype(bf16), pw_ref[...]).astype(bf16)

    return ([shifted_copies] + [functools.partial(conv_chunk, c) for c in range(TM // _CONF_CHUNK)]
            + [finish])


def _short_conv_tasks(pad_ref, w_ref, b_ref, outs):
    def group(g):
        cs = slice(g * W_BR, (g + 1) * W_BR)
        acc = jnp.broadcast_to(b_ref[:, cs], (TM, W_BR))
        for j in range(SHORT_K):
            r0 = _CONF_HB + j - SHORT_K // 2
            acc = acc + w_ref[j:j + 1, cs] * pad_ref[r0:r0 + TM, cs]
        outs[g][...] = acc

    return [functools.partial(group, g) for g in range(len(outs))]


def _pool_tasks(pad_ref, pw_ref, ps_ref, o_ref):
    def group(g):
        is_lat, jt = _tile_position()
        seq_len = jnp.where(is_lat, DEC_SEQ, SEQ)
        t = jnp.where(is_lat, jt * TM, 0) + lax.broadcasted_iota(jnp.int32, (TM, POOL_GROUP), 0)
        cs = slice(g * POOL_GROUP, (g + 1) * POOL_GROUP)
        hw = POOL_WINDOWS[g] // 2
        s = pad_ref[_CONF_HB - hw:_CONF_HB - hw + TM, cs]
        for d in range(-hw + 1, hw):
            s = s + pad_ref[_CONF_HB + d:_CONF_HB + d + TM, cs]
        cnt = (jnp.minimum(t + hw, seq_len) - jnp.maximum(t - hw, 0)).astype(f32)
        pooled = s / cnt - pad_ref[_CONF_HB:_CONF_HB + TM, cs]
        y = _mm(pooled.astype(bf16), pw_ref[g])
        o_ref[:, cs] = (y * ps_ref[:, cs]).astype(bf16)

    return [functools.partial(group, g) for g in range(len(POOL_WINDOWS))]


def _inproj_kernel(xp_ref, xc_ref, xl_ref, xn_ref, mod_ref, g_ref, w_ref,
                   dw_ref, dwb_ref, lng_ref, lnb_ref, cpw_ref, sw_ref, sbias_ref, ppw_ref, pps_ref,
                   pa_ref, sa_ref, v_ref, x1_ref, x2_ref, sb_ref, pc_ref, sc_ref, q_ref, kv_ref, sd_ref, gm_ref,
                   pad_a, shift_a, conv_a, pad_b, pad_c):
    has_prev, has_next = _halo_flags()
    gain = g_ref[...]
    scale1 = 1.0 + mod_ref[1:2, :]
    shift = mod_ref[0:1, :]

    def modulated(x):
        ms = jnp.mean(x * x, axis=-1, keepdims=True)
        return (x * lax.rsqrt(ms + EPS) * gain * scale1 + shift).astype(bf16)

    h = modulated(jnp.where(pl.program_id(0) < CTX_TILES, xc_ref[...], xl_ref[...]))
    h_ext = jnp.concatenate([modulated(xp_ref[...]), h, modulated(xn_ref[...])], axis=0)
    rows = TM + 2 * _CONF_HB
    r = lax.broadcasted_iota(jnp.int32, (rows, 1), 0)
    valid = jnp.logical_and(jnp.logical_or(r >= _CONF_HB, has_prev),
                            jnp.logical_or(r < _CONF_HB + TM, has_next))

    def proj(c0, width):
        return _mm(h, w_ref[:, c0:c0 + width])

    def proj_ext(c0, width):
        return _mm(h_ext, w_ref[:, c0:c0 + width])

    def ext_into(dst, c0, col):
        def task():
            dst[:, col * W_BR:(col + 1) * W_BR] = jnp.where(valid, proj_ext(c0, W_BR), 0.0)
        return task

    def act_into(dst, c0, col, act, width=W_BR):
        def task():
            y = proj(c0, width)
            dst[:, col * W_BR:col * W_BR + width] = (y if act is None else act(y)).astype(dst.dtype)
        return task

    pad_a[...] = jnp.where(valid, proj_ext(_C_GLU, W_BR) * _sigmoid(proj_ext(_C_GLU + W_BR, W_BR)), 0.0)

    mxu_tasks = [ext_into(pad_b, _C_BPROJ + j * W_BR, j) for j in range(3)] + [ext_into(pad_c, _C_CIN, 0)]
    mxu_tasks += [act_into(sa_ref, _C_AGATE, 0, _silu), act_into(sb_ref, _C_BGATE, 0, _silu),
                  act_into(sc_ref, _C_CGATE, 0, _silu)]
    mxu_tasks += [act_into(q_ref, _C_Q + j * W_BR, j, None) for j in range(W_ATTN // W_BR)]
    mxu_tasks += [act_into(kv_ref, _C_KV, 0, None, 2 * W_KV)]
    mxu_tasks += [act_into(sd_ref, _C_DGATE + j * W_BR, j, _silu) for j in range(W_ATTN // W_BR)]
    mxu_tasks += [act_into(gm_ref, _C_GM + j * W_BR, j, _sigmoid) for j in range(4 * D_MODEL // W_BR)]

    vpu_tasks = _conformer_tasks(pad_a, shift_a, conv_a, dw_ref, dwb_ref, lng_ref, lnb_ref, cpw_ref, pa_ref)
    n_ext = 4
    later = (_short_conv_tasks(pad_b, sw_ref, sbias_ref, (v_ref, x1_ref, x2_ref))
             + _pool_tasks(pad_c, ppw_ref, pps_ref, pc_ref))
    vpu_tasks = vpu_tasks[:n_ext + 1] + later + vpu_tasks[n_ext + 1:]
    for k in range(max(len(mxu_tasks), len(vpu_tasks))):
        if k < len(mxu_tasks):
            mxu_tasks[k]()
        if k < len(vpu_tasks):
            vpu_tasks[k]()


def _inproj_call(x_ctx, x_lat, layer, mod_tiles_l, norm_g_l, w_in_bf, dw_w, dw_b, ln_g, ln_b, conv_pw_bf,
                 short_w, short_b, pool_w_bf, pool_scale):
    widths = [(W_BR, bf16), (W_BR, bf16), (W_BR, f32), (W_BR, f32), (W_BR, f32), (W_BR, bf16),
              (W_BR, bf16), (W_BR, bf16), (W_ATTN, f32), (2 * W_KV, f32), (W_ATTN, bf16), (4 * D_MODEL, bf16)]
    row = lambda v: v.reshape(1, -1)
    rep = lambda v: jnp.broadcast_to(v[..., None, :], v.shape[:-1] + (SUBLANES, W_BR))
    const = lambda shape: pl.BlockSpec(shape, lambda i: (0,) * len(shape))
    ext_rows = TM + 2 * _CONF_HB
    r = TM // _CONF_HB
    last_halo = N_LAT_TOK // _CONF_HB - 1
    lat_tile = lambda i: jnp.maximum(i - CTX_TILES, 0)
    return pl.pallas_call(
        _inproj_kernel,
        grid=(N_TILES,),
        in_specs=[
            pl.BlockSpec((_CONF_HB, D_MODEL), lambda i: (jnp.maximum(lat_tile(i) * r - 1, 0), 0)),
            pl.BlockSpec((TM, D_MODEL), lambda i: (jnp.minimum(i, CTX_TILES - 1), 0)),
            pl.BlockSpec((TM, D_MODEL), lambda i: (lat_tile(i), 0)),
            pl.BlockSpec((_CONF_HB, D_MODEL), lambda i: (jnp.minimum((lat_tile(i) + 1) * r, last_halo), 0)),
            pl.BlockSpec((None, 3, D_MODEL), lambda i: (i, 0, 0)),
            const((1, D_MODEL)),
            pl.BlockSpec((None, D_MODEL, N_IN), lambda i: (layer, 0, 0), pipeline_mode=pl.Buffered(1)),
            const((CONV_K, SUBLANES, W_BR)), const((SUBLANES, W_BR)), const((1, W_BR)), const((1, W_BR)),
            const((W_BR, W_BR)), const((SHORT_K, 3 * W_BR)), const((1, 3 * W_BR)),
            const((len(POOL_WINDOWS), POOL_GROUP, POOL_GROUP)), const((1, W_BR))],
        out_specs=[pl.BlockSpec((TM, w), lambda i: (i, 0)) for w, _ in widths],
        out_shape=[jax.ShapeDtypeStruct((N_TOK, w), dt) for w, dt in widths],
        scratch_shapes=[pltpu.VMEM((ext_rows, W_BR), f32), pltpu.VMEM((SUBLANES, _CONF_SPAN, W_BR), f32),
                        pltpu.VMEM((TM, W_BR), f32), pltpu.VMEM((ext_rows, 3 * W_BR), f32),
                        pltpu.VMEM((ext_rows, W_BR), f32)],
        compiler_params=_params(1, VMEM_LIMIT),
        name="in_proj",
    )(x_lat, x_ctx, x_lat, x_lat, mod_tiles_l, row(norm_g_l), w_in_bf, rep(dw_w), rep(dw_b), row(ln_g), row(ln_b), conv_pw_bf,
      short_w, row(short_b), pool_w_bf, row(pool_scale))


def _dft_tables(n2):
    n = DFT_B * n2
    k1 = np.arange(DFT_B, dtype=np.int64)[:, None]
    t1 = np.arange(DFT_B, dtype=np.int64)[None, :]
    fwd, inv = [], []
    for k2 in range(n2):
        ang = -2.0 * np.pi * (((n2 * k1 + k2) * t1) % n).astype(np.float64) / n
        gr, gi = np.cos(ang), np.sin(ang)
        fwd.append(np.concatenate([gr, gi], axis=0))
        inv.append(np.concatenate([gr.T, gi.T], axis=1))
    return (jnp.asarray(np.stack(fwd).astype(np.float32)).astype(bf16),
            jnp.asarray(np.stack(inv).astype(np.float32)).astype(bf16))


def _cmul_const(x, ang):
    c, s = math.cos(ang), math.sin(ang)
    xr, xi = x
    return (c * xr - s * xi, s * xr + c * xi)


def _fft_list(xs, sign):
    n = len(xs)
    if n == 1:
        return list(xs)
    ev = _fft_list(xs[0::2], sign)
    od = _fft_list(xs[1::2], sign)
    out = [None] * n
    for k in range(n // 2):
        orr, oi = od[k]
        if k == 0:
            tr, ti = orr, oi
        elif 4 * k == n:
            tr, ti = (-oi, orr) if sign > 0 else (oi, -orr)
        else:
            tr, ti = _cmul_const(od[k], sign * 2.0 * math.pi * k / n)
        er, ei = ev[k]
        out[k] = (er + tr, ei + ti)
        out[k + n // 2] = (er - tr, ei - ti)
    return out


def _fft_padded(xs, sign):
    m = len(xs)
    ev = _fft_list(xs, sign)
    od = _fft_list([xs[0]] + [_cmul_const(xs[t], sign * math.pi * t / m) for t in range(1, m)], sign)
    out = []
    for j in range(m):
        out += [ev[j], od[j]]
    return out


def _ifft_truncated(ws):
    m = len(ws) // 2
    ev = _fft_list(ws[0::2], 1)
    od = _fft_list(ws[1::2], 1)
    out = []
    for t in range(m):
        tr, ti = od[t] if t == 0 else _cmul_const(od[t], math.pi * t / m)
        out.append((ev[t][0] + tr, ev[t][1] + ti))
    return out


def _spectral_block(u, m, mt, hr, hi):
    w = hr.shape[1]
    hr, hi = hr.astype(f32), hi.astype(f32)
    p = _mm(m, u)
    xr = p[:DFT_B, :w] - p[DFT_B:, w:]
    xi = p[:DFT_B, w:] + p[DFT_B:, :w]
    yr = xr * hr - xi * hi
    yi = xr * hi + xi * hr
    rhs = jnp.concatenate([jnp.concatenate([yr, yi], axis=1),
                           jnp.concatenate([yi, -yr], axis=1)], axis=0).astype(bf16)
    return _mm(mt, rhs)


def _slab(s):
    return pl.ds(pl.multiple_of(s * SUBLANES, SUBLANES), SUBLANES)


_FEAT_PAD = 64


def _filter_features(seq_len):
    m = np.arange(2 * seq_len)
    j = np.where(m < seq_len, m, 2 * seq_len - m).astype(np.float64)
    t = j / (seq_len - 1)
    bands = np.linspace(1e-4, FILTER_BANDS - 1, FILTER_BANDS)[None, :]
    w = (2.0 * math.pi / seq_len) * j[:, None]
    z = np.concatenate([t[:, None], np.cos(bands * w), np.sin(bands * w)], axis=-1)
    z = np.pad(z, ((0, 0), (0, _FEAT_PAD - FILTER_EMB)))
    return jnp.asarray(z.astype(np.float32)), jnp.asarray(t.astype(np.float32)[:, None])


def _abs_deltas():
    max_decay = math.log(DECAY_TARGET) / FAST_DECAY_PCT
    min_decay = math.log(DECAY_TARGET) / SLOW_DECAY_PCT
    d = np.abs(np.linspace(min_decay, max_decay, W_BR))
    return jnp.asarray(np.concatenate([d, d]).astype(np.float32)[None, :])


def _filter_mlp_kernel(seq_len, tr, z_ref, t_ref, w1_ref, b1_ref, fr_ref, w2_ref, b2_ref, w3_ref, b3_ref,
                       ad_ref, h_ref, ss_ref):
    i = pl.program_id(0)
    fr = fr_ref[...]
    hdn = jnp.sin(fr * (_mm3(z_ref[...], w1_ref[...]) + b1_ref[...]))
    hdn = jnp.sin(fr * (_mm3(hdn, w2_ref[...]) + b2_ref[...]))
    h = _mm3(hdn, w3_ref[...]) + b3_ref[...]
    h = h * jnp.exp(-t_ref[...] * ad_ref[...])
    m = i * tr + lax.broadcasted_iota(jnp.int32, h.shape, 0)
    h = jnp.where(m == seq_len, 0.0, h)
    h_ref[...] = h

    @pl.when(i == 0)
    def _():
        ss_ref[...] = jnp.zeros_like(ss_ref)

    ss_ref[...] += jnp.sum(h * h, axis=0, keepdims=True)


def _filter_mlp_call(seq_len, feats, tcol, w1p, b1, freq, w2, b2, w3, b3, absd):
    tr = min(512, seq_len)
    steps = 2 * seq_len // tr
    half_steps = seq_len // tr
    wide = 2 * W_BR
    const = lambda shape: pl.BlockSpec(shape, lambda i: (0, 0))
    row = lambda v: v.reshape(1, -1)
    return pl.pallas_call(
        functools.partial(_filter_mlp_kernel, seq_len, tr),
        grid=(steps,),
        in_specs=[
            pl.BlockSpec((tr, _FEAT_PAD), lambda i: (i, 0)),
            pl.BlockSpec((tr, 1), lambda i: (i, 0)),
            const((_FEAT_PAD, FILTER_HIDDEN)), const((1, FILTER_HIDDEN)), const((1, FILTER_HIDDEN)),
            const((FILTER_HIDDEN, FILTER_HIDDEN)), const((1, FILTER_HIDDEN)),
            pl.BlockSpec((FILTER_HIDDEN, wide), lambda i: (0, i // half_steps)),
            pl.BlockSpec((1, wide), lambda i: (0, i // half_steps)),
            const((1, wide)),
        ],
        out_specs=[pl.BlockSpec((tr, wide), lambda i: (i, 0)), const((1, wide))],
        out_shape=[jax.ShapeDtypeStruct((2 * seq_len, wide), f32), jax.ShapeDtypeStruct((1, wide), f32)],
        compiler_params=_params(1),
        name="filter_mlp",
    )(feats, tcol, w1p, row(b1), row(freq), w2, row(b2), w3, row(b3), absd)


_RADIX_ROWS = 128
_K2_PER_STEP = 4


def _filter_radix_kernel(n2, h_ref, ur_ref, ui_ref, sr_ref, si_ref):
    zero = jnp.zeros((SUBLANES, LANES), f32)

    def body(s, carry):
        rs = _slab(s)
        us = _fft_list([(h_ref[t, rs, :], zero) for t in range(n2)], -1)
        for k in range(n2):
            sr_ref[k, rs, :] = us[k][0]
            si_ref[k, rs, :] = us[k][1]
        return carry

    lax.fori_loop(0, _RADIX_ROWS // SUBLANES, body, 0)
    for k in range(n2):
        ur_ref[k] = sr_ref[k].astype(bf16)
        ui_ref[k] = si_ref[k].astype(bf16)


def _filter_spec_kernel(scale_const, ur_ref, ui_ref, m_ref, ss_ref, hr_ref, hi_ref):
    w = ur_ref.shape[2]
    scale = lax.rsqrt(ss_ref[...] + EPS) * scale_const
    for k in range(ur_ref.shape[0]):
        p = _mm(m_ref[k], jnp.concatenate([ur_ref[k], ui_ref[k]], axis=1))
        hr_ref[k] = ((p[:DFT_B, :w] - p[DFT_B:, w:]) * scale).astype(hr_ref.dtype)
        hi_ref[k] = ((p[:DFT_B, w:] + p[DFT_B:, :w]) * scale).astype(hi_ref.dtype)


def _filter_spectrum(n2, mfwd, hfilt, ss):
    wide = 2 * W_BR
    blk = pl.BlockSpec((n2, _RADIX_ROWS, LANES), lambda r, c: (0, r, c))
    u_shape = jax.ShapeDtypeStruct((n2, DFT_B, wide), bf16)
    ur, ui = pl.pallas_call(
        functools.partial(_filter_radix_kernel, n2),
        grid=(DFT_B // _RADIX_ROWS, wide // LANES),
        in_specs=[blk],
        out_specs=[blk, blk],
        out_shape=[u_shape, u_shape],
        scratch_shapes=[pltpu.VMEM((n2, _RADIX_ROWS, LANES), f32)] * 2,
        compiler_params=_params(2),
        name="filter_radix",
    )(hfilt.reshape(n2, DFT_B, wide))
    kb = min(_K2_PER_STEP, n2)
    kspec = pl.BlockSpec((kb, DFT_B, wide), lambda k: (k, 0, 0))
    h_shape = jax.ShapeDtypeStruct((n2, DFT_B, wide), bf16)
    return pl.pallas_call(
        functools.partial(_filter_spec_kernel, 1.0 / (n2 * DFT_B)),
        grid=(n2 // kb,),
        in_specs=[kspec, kspec, pl.BlockSpec((kb, 2 * DFT_B, DFT_B), lambda k: (k, 0, 0)),
                  pl.BlockSpec((1, wide), lambda k: (0, 0))],
        out_specs=[kspec, kspec],
        out_shape=[h_shape, h_shape],
        compiler_params=_params(1),
        name="filter_spec",
    )(ur, ui, mfwd, ss)


def _hy_ctx_kernel(va_ref, vb_ref, x1a_ref, x1b_ref, x2a_ref, x2b_ref, m_ref, mt_ref, hr_ref, hi_ref,
                   skip_ref, o_ref):
    n2 = m_ref.shape[0]
    za, zb = va_ref[...], vb_ref[...]
    gates = ((x1a_ref, x1b_ref), (x2a_ref, x2b_ref))
    for order in range(2):
        cs = slice(order * W_BR, (order + 1) * W_BR)
        u = jnp.concatenate([za, zb], axis=1).astype(bf16)
        w = _spectral_block(u, m_ref[0], mt_ref[0], hr_ref[0, :, cs], hi_ref[0, :, cs])
        for k2 in range(1, n2):
            w = w + _spectral_block(u, m_ref[k2], mt_ref[k2], hr_ref[k2, :, cs], hi_ref[k2, :, cs])
        skip = skip_ref[order:order + 1, :]
        za = gates[order][0][...] * (w[:, :W_BR] + za * skip)
        zb = gates[order][1][...] * (w[:, W_BR:] + zb * skip)
    o_ref[0:DFT_B, :] = za
    o_ref[DFT_B:, :] = zb


def _hy_ctx_call(tabs, spec, v, x1, x2, skip):
    mfwd, minv = tabs
    n2 = mfwd.shape[0]
    assert n2 == 2
    seq = lambda par: pl.BlockSpec((DFT_B, W_BR), lambda p: (2 * p + par, 0))
    whole = lambda a: pl.BlockSpec(a.shape, lambda p: (0,) * a.ndim)
    return pl.pallas_call(
        _hy_ctx_kernel,
        grid=(BATCH // 2,),
        in_specs=[seq(0), seq(1), seq(0), seq(1), seq(0), seq(1), whole(mfwd), whole(minv),
                  whole(spec[0]), whole(spec[1]), whole(skip)],
        out_specs=pl.BlockSpec((2 * DFT_B, W_BR), lambda p: (p, 0)),
        out_shape=jax.ShapeDtypeStruct((N_CTX_TOK, W_BR), f32),
        compiler_params=_params(1, VMEM_LIMIT),
        name="hyena_ctx",
    )(v, v, x1, x1, x2, x2, mfwd, minv, spec[0], spec[1], skip)


def _store_radix(us, rs, sr_ref, si_ref):
    for k, (ur, ui) in enumerate(us):
        sr_ref[k, rs, :] = ur
        si_ref[k, rs, :] = ui


def _cast_radix(sr_ref, si_ref, ur_ref, ui_ref):
    for k in range(sr_ref.shape[0]):
        ur_ref[k] = sr_ref[k].astype(bf16)
        ui_ref[k] = si_ref[k].astype(bf16)


def _hy_fwd_radix_kernel(m, za_ref, zb_ref, ur_ref, ui_ref, sr_ref, si_ref):
    def body(s, carry):
        rs = _slab(s)
        _store_radix(_fft_padded([(za_ref[t, rs, :], zb_ref[t, rs, :]) for t in range(m)], -1),
                     rs, sr_ref, si_ref)
        return carry

    lax.fori_loop(0, _RADIX_ROWS // SUBLANES, body, 0)
    _cast_radix(sr_ref, si_ref, ur_ref, ui_ref)


def _hy_spec_kernel(ur_ref, ui_ref, m_ref, mt_ref, hr_ref, hi_ref, w_ref):
    for k in range(ur_ref.shape[0]):
        u = jnp.concatenate([ur_ref[k], ui_ref[k]], axis=1)
        w_ref[k] = _spectral_block(u, m_ref[k], mt_ref[k], hr_ref[k], hi_ref[k]).astype(w_ref.dtype)


def _hy_inv_radix_kernel(m, final, wr_ref, wi_ref, za_ref, zb_ref, ga_ref, gb_ref, skip_ref, o_ref, *rest):
    skip = skip_ref[...]
    sr_ref, si_ref = rest[-2:]
    for k in range(2 * m):
        sr_ref[k] = wr_ref[k].astype(f32)
        si_ref[k] = wi_ref[k].astype(f32)

    def body(s, carry):
        rs = _slab(s)
        ys = _ifft_truncated([(sr_ref[k, rs, :], si_ref[k, rs, :]) for k in range(2 * m)])
        zs = []
        for t in range(m):
            za = ga_ref[t, rs, :] * (ys[t][0] + za_ref[t, rs, :] * skip)
            zb = gb_ref[t, rs, :] * (ys[t][1] + zb_ref[t, rs, :] * skip)
            o_ref[0, t, rs, :] = za
            o_ref[1, t, rs, :] = zb
            zs.append((za, zb))
        if not final:
            _store_radix(_fft_padded(zs, -1), rs, sr_ref, si_ref)
        return carry

    lax.fori_loop(0, _RADIX_ROWS // SUBLANES, body, 0)
    if not final:
        _cast_radix(sr_ref, si_ref, rest[0], rest[1])


def _hy_lat_call(tabs, spec, v, x1, x2, skip):
    mfwd, minv = tabs
    n2 = mfwd.shape[0]
    m = n2 // 2
    assert m * DFT_B == DEC_SEQ and DEC_BATCH == 2
    seq0 = N_CTX_TOK // DEC_SEQ
    grid = (DFT_B // _RADIX_ROWS, W_BR // LANES)

    def tok(arr, b):
        return (arr.reshape(N_TOK // DEC_SEQ, m, DFT_B, W_BR),
                pl.BlockSpec((None, m, _RADIX_ROWS, LANES), lambda r, c: (seq0 + b, 0, r, c)))

    def pair(arr, b):
        return (arr, pl.BlockSpec((None, m, _RADIX_ROWS, LANES), lambda r, c: (b, 0, r, c)))

    u_spec = pl.BlockSpec((n2, _RADIX_ROWS, LANES), lambda r, c: (0, r, c))
    u_shape = jax.ShapeDtypeStruct((n2, DFT_B, W_BR), bf16)
    z_spec = pl.BlockSpec((2, m, _RADIX_ROWS, LANES), lambda r, c: (0, 0, r, c))
    z_shape = jax.ShapeDtypeStruct((2, m, DFT_B, W_BR), f32)
    scratch = [pltpu.VMEM((n2, _RADIX_ROWS, LANES), f32)] * 2

    def spectral(ur, ui, order):
        kb = min(_K2_PER_STEP, n2)
        kspec = pl.BlockSpec((kb, DFT_B, W_BR), lambda k: (k, 0, 0))
        hspec = pl.BlockSpec((kb, DFT_B, W_BR), lambda k: (k, 0, order))
        return pl.pallas_call(
            _hy_spec_kernel,
            grid=(n2 // kb,),
            in_specs=[kspec, kspec, pl.BlockSpec((kb, 2 * DFT_B, DFT_B), lambda k: (k, 0, 0)),
                      pl.BlockSpec((kb, DFT_B, 2 * DFT_B), lambda k: (k, 0, 0)), hspec, hspec],
            out_specs=pl.BlockSpec((kb, DFT_B, 2 * W_BR), lambda k: (k, 0, 0)),
            out_shape=jax.ShapeDtypeStruct((n2, DFT_B, 2 * W_BR), bf16),
            compiler_params=_params(1),
            name="hyena_spec",
        )(ur, ui, mfwd, minv, spec[0], spec[1])

    def inverse(w, z_srcs, gate, order, final):
        wr_spec = pl.BlockSpec((n2, _RADIX_ROWS, LANES), lambda r, c: (0, r, c))
        wi_spec = pl.BlockSpec((n2, _RADIX_ROWS, LANES), lambda r, c: (0, r, W_BR // LANES + c))
        g_srcs = [tok(gate, 0), tok(gate, 1)]
        return pl.pallas_call(
            functools.partial(_hy_inv_radix_kernel, m, final),
            grid=grid,
            in_specs=[wr_spec, wi_spec, z_srcs[0][1], z_srcs[1][1], g_srcs[0][1], g_srcs[1][1],
                      pl.BlockSpec((1, LANES), lambda r, c: (0, c))],
            out_specs=z_spec if final else [z_spec, u_spec, u_spec],
            out_shape=z_shape if final else [z_shape, u_shape, u_shape],
            scratch_shapes=scratch,
            compiler_params=_params(2, VMEM_LIMIT),
            name="hyena_inv_radix",
        )(w, w, z_srcs[0][0], z_srcs[1][0], g_srcs[0][0], g_srcs[1][0], skip[order].reshape(1, W_BR))

    v_srcs = [tok(v, 0), tok(v, 1)]
    ur, ui = pl.pallas_call(
        functools.partial(_hy_fwd_radix_kernel, m),
        grid=grid,
        in_specs=[v_srcs[0][1], v_srcs[1][1]],
        out_specs=[u_spec, u_spec],
        out_shape=[u_shape, u_shape],
        scratch_shapes=scratch,
        compiler_params=_params(2),
        name="hyena_fwd_radix",
    )(v_srcs[0][0], v_srcs[1][0])
    z1, ur, ui = inverse(spectral(ur, ui, 0), v_srcs, x1, 0, False)
    return inverse(spectral(ur, ui, 1), [pair(z1, 0), pair(z1, 1)], x2, 1, True)


_QK_SCALE = HEAD_DIM ** -0.5 * math.log2(math.e)


def _rope_tables():
    half = HEAD_DIM // 2
    f = half // 2
    inv = ROPE_THETA ** (-np.arange(f, dtype=np.float64) / f)
    pos = np.arange(DEC_SEQ)
    row, col = (pos // GRID_W).astype(np.float64), (pos % GRID_W).astype(np.float64)
    ang = np.concatenate([np.tile(row[:, None] * inv[None, :], (1, 2)),
                          np.tile(col[:, None] * inv[None, :], (1, 2))], axis=1)
    sign = np.tile(np.concatenate([-np.ones(f), np.ones(f)]), 2)[None, :]
    cos = np.concatenate([np.ones((TM, HEAD_DIM)), np.cos(ang)], axis=0)
    sin = np.concatenate([np.zeros((TM, HEAD_DIM)), np.sin(ang) * sign], axis=0)
    rep = W_KV // HEAD_DIM
    return (jnp.asarray(np.tile(cos, (1, rep)).astype(np.float32)),
            jnp.asarray(np.tile(sin, (1, rep)).astype(np.float32)))


def _group_sum_matrix():
    idx = np.arange(W_KV) // HEAD_DIM
    return jnp.asarray((idx[:, None] == idx[None, :]).astype(np.float32)).astype(bf16)


def _qk_prep_kernel(q_ref, kv_ref, gq_ref, gk_ref, cos_ref, sin_ref, bd_ref,
                    qt_ref, kn_ref, khm_ref, vthm_ref):
    cos = cos_ref[...]
    sin = sin_ref[...]
    bd = bd_ref[...]
    eye = (lax.broadcasted_iota(jnp.int32, (HEAD_DIM, HEAD_DIM), 0)
           == lax.broadcasted_iota(jnp.int32, (HEAD_DIM, HEAD_DIM), 1)).astype(f32).astype(bf16)
    lane = lax.broadcasted_iota(jnp.int32, (TM, W_KV), 1)
    first_half = jnp.bitwise_and(lane, HEAD_DIM // 2 - 1) < HEAD_DIM // 4
    quarter = HEAD_DIM // 4

    def norm(x, g):
        hi, lo = _split_bf16(x * x)
        ss = _mm(hi, bd) + _mm(lo, bd)
        return x * lax.rsqrt(ss * (1.0 / HEAD_DIM) + EPS) * g

    def transposed(x):
        return lax.dot_general(eye, x.astype(bf16), (((1,), (1,)), ((), ())),
                               preferred_element_type=f32).astype(bf16)

    def rope(x):
        swapped = jnp.where(first_half, pltpu.roll(x, W_KV - quarter, 1), pltpu.roll(x, quarter, 1))
        return x * cos + swapped * sin

    for j in range(W_ATTN // W_KV):
        cs = slice(j * W_KV, (j + 1) * W_KV)
        qs = rope(norm(q_ref[:, cs], gq_ref[...]))
        for hh in range(N_KV_HEADS):
            qt_ref[j * N_KV_HEADS + hh] = transposed(qs[:, hh * HEAD_DIM:(hh + 1) * HEAD_DIM])
    kn = norm(kv_ref[:, :W_KV], gk_ref[...])
    kn_ref[...] = kn
    kr = rope(kn)
    v = kv_ref[:, W_KV:]
    for g in range(N_KV_HEADS):
        hs = slice(g * HEAD_DIM, (g + 1) * HEAD_DIM)
        khm_ref[g] = kr[:, hs].astype(bf16)
        vthm_ref[g] = transposed(v[:, hs])


def _qk_prep_call(q, kv, q_norm, k_norm, cos_t, sin_t, bd):
    rep = W_KV // HEAD_DIM
    const = lambda shape: pl.BlockSpec(shape, lambda i: (0, 0))
    tab = pl.BlockSpec((TM, W_KV), lambda i: (jnp.where(i < CTX_TILES, 0, 1 + lax.rem(i, LAT_TPS)), 0))
    tile = lambda w: pl.BlockSpec((TM, w), lambda i: (i, 0))
    return pl.pallas_call(
        _qk_prep_kernel,
        grid=(N_TILES,),
        in_specs=[tile(W_ATTN), tile(2 * W_KV), const((1, W_KV)), const((1, W_KV)), tab, tab,
                  const((W_KV, W_KV))],
        out_specs=[pl.BlockSpec((N_Q_HEADS, HEAD_DIM, TM), lambda i: (0, 0, i)), tile(W_KV),
                   pl.BlockSpec((N_KV_HEADS, TM, HEAD_DIM), lambda i: (0, i, 0)),
                   pl.BlockSpec((N_KV_HEADS, HEAD_DIM, TM), lambda i: (0, 0, i))],
        out_shape=[jax.ShapeDtypeStruct((N_Q_HEADS, HEAD_DIM, N_TOK), bf16),
                   jax.ShapeDtypeStruct((N_TOK, W_KV), f32),
                   jax.ShapeDtypeStruct((N_KV_HEADS, N_TOK, HEAD_DIM), bf16),
                   jax.ShapeDtypeStruct((N_KV_HEADS, HEAD_DIM, N_TOK), bf16)],
        compiler_params=_params(1),
        name="qk_prep",
    )(q, kv, (jnp.tile(q_norm, rep) * _QK_SCALE).reshape(1, W_KV), jnp.tile(k_norm, rep).reshape(1, W_KV),
      cos_t, sin_t, bd)


_ATT_CK = 256


def _attn_kernel(hps, has_cache, seq_len, qt_ref, k_ref, vt_ref, *rest):
    if has_cache:
        kc_ref, vct_ref, o_ref, s_ref = rest
    else:
        o_ref, s_ref = rest
    chunks = ([(True, c) for c in range(PAST_LEN // _ATT_CK)] if has_cache else [])
    chunks += [(False, c) for c in range(seq_len // _ATT_CK)]
    tq = qt_ref.shape[2]
    groups = _ATT_CK // SUBLANES

    def span(c):
        return slice(c * _ATT_CK, (c + 1) * _ATT_CK)

    def k_chunk(g, chunk):
        cached, c = chunk
        return kc_ref[g, span(c), :] if cached else k_ref[g, span(c), :]

    def vt_chunk(g, chunk):
        cached, c = chunk
        return vct_ref[g, :, span(c)] if cached else vt_ref[g, :, span(c)]

    def score_chunk(h, i, m8):
        s_ref[h % 2, span(i), :] = _mm(k_chunk(h // Q_PER_KV, chunks[i]), qt_ref[h])
        return jnp.maximum(m8, jnp.max(s_ref[h % 2, span(i), :].reshape(groups, SUBLANES, tq), axis=0))

    neg = jnp.full((SUBLANES, tq), -jnp.inf, f32)
    m8 = neg
    for i in range(len(chunks)):
        m8 = score_chunk(0, i, m8)
    for h in range(hps):
        m = jnp.max(m8, axis=0, keepdims=True)
        m8 = neg
        l8 = jnp.zeros((SUBLANES, tq), f32)
        acc = jnp.zeros((HEAD_DIM, tq), f32)
        for i in range(len(chunks)):
            if h + 1 < hps:
                m8 = score_chunk(h + 1, i, m8)
            p = jnp.exp2(s_ref[h % 2, span(i), :] - m)
            l8 = l8 + jnp.sum(p.reshape(groups, SUBLANES, tq), axis=0)
            acc = acc + _mm(vt_chunk(h // Q_PER_KV, chunks[i]), p.astype(bf16))
        l = jnp.sum(l8, axis=0, keepdims=True)
        o_ref[:, h * HEAD_DIM:(h + 1) * HEAD_DIM] = (acc / l).T.astype(bf16)


def _attn_call(qt, khm, vthm, n_seq, seq_len, hps, tq, tok0, cache=None):
    kvb = hps // Q_PER_KV
    qtiles = seq_len // tq
    row0 = tok0 // tq
    seq0 = tok0 // seq_len
    in_specs = [
        pl.BlockSpec((hps, HEAD_DIM, tq), lambda b, g, t: (g, 0, row0 + b * qtiles + t)),
        pl.BlockSpec((kvb, seq_len, HEAD_DIM), lambda b, g, t: (g, seq0 + b, 0)),
        pl.BlockSpec((kvb, HEAD_DIM, seq_len), lambda b, g, t: (g, 0, seq0 + b)),
    ]
    args = [qt, khm, vthm]
    if cache is not None:
        in_specs += [pl.BlockSpec((None, kvb, PAST_LEN, HEAD_DIM), lambda b, g, t: (b, g, 0, 0)),
                     pl.BlockSpec((None, kvb, HEAD_DIM, PAST_LEN), lambda b, g, t: (b, g, 0, 0))]
        args += list(cache)
    return pl.pallas_call(
        functools.partial(_attn_kernel, hps, cache is not None, seq_len),
        grid=(n_seq, N_Q_HEADS // hps, qtiles),
        in_specs=in_specs,
        out_specs=pl.BlockSpec((tq, hps * HEAD_DIM), lambda b, g, t: (b * qtiles + t, g)),
        out_shape=jax.ShapeDtypeStruct((n_seq * seq_len, W_ATTN), bf16),
        scratch_shapes=[pltpu.VMEM((2, seq_len + (PAST_LEN if cache is not None else 0), tq), f32)],
        compiler_params=_params(3, VMEM_LIMIT),
        name="attention",
    )(*args)


def _merge_kernel(xc_ref, xl_ref, mod_ref, pa_ref, pbc_ref, pbl_ref, pc_ref, attc_ref, attl_ref,
                  sa_ref, sb_ref, sc_ref, sd_ref, gm_ref,
                  woa_ref, wob_ref, woc_ref, wod_ref, wout_ref, oc_ref, ol_ref):
    is_ctx = pl.program_id(0) < CTX_TILES

    def gated(p, s_ref):
        return (p.astype(f32) * s_ref[...].astype(f32)).astype(bf16)

    def gm(j):
        return gm_ref[:, j * D_MODEL:(j + 1) * D_MODEL].astype(f32)

    pb = jnp.where(is_ctx, pbc_ref[...], pbl_ref[...])
    att = jnp.where(is_ctx, attc_ref[...], attl_ref[...])
    merged = gm(0) * _mm(gated(pa_ref[...], sa_ref), woa_ref[...])
    merged = merged + gm(1) * _mm(gated(pb, sb_ref), wob_ref[...])
    merged = merged + gm(2) * _mm(gated(pc_ref[...], sc_ref), woc_ref[...])
    merged = merged + gm(3) * _mm(gated(att, sd_ref), wod_ref[...])
    y = _mm(merged.astype(bf16), wout_ref[...])
    out = jnp.where(is_ctx, xc_ref[...], xl_ref[...]) + mod_ref[2:3, :] * y
    ol_ref[...] = out

    @pl.when(is_ctx)
    def _():
        oc_ref[...] = out


def _merge_call(x_ctx, x_lat, mod_tiles_l, pa, pb_ctx, pb_lat, pc, att_ctx, att_lat, sa, sb, sc, sd, gm,
                woa, wob, woc, wod, wout):
    tile = lambda w: pl.BlockSpec((TM, w), lambda i: (i, 0))
    ctx_tile = lambda w: pl.BlockSpec((TM, w), lambda i: (jnp.minimum(i, CTX_TILES - 1), 0))
    lat_tile = lambda w: pl.BlockSpec((TM, w), lambda i: (jnp.maximum(i - CTX_TILES, 0), 0))
    const = lambda shape: pl.BlockSpec(shape, lambda i: (0, 0))
    return pl.pallas_call(
        _merge_kernel,
        grid=(N_TILES,),
        in_specs=[ctx_tile(D_MODEL), lat_tile(D_MODEL), pl.BlockSpec((None, 3, D_MODEL), lambda i: (i, 0, 0)),
                  tile(W_BR), ctx_tile(W_BR), lat_tile(W_BR), tile(W_BR),
                  ctx_tile(W_ATTN), lat_tile(W_ATTN),
                  tile(W_BR), tile(W_BR), tile(W_BR), tile(W_ATTN), tile(4 * D_MODEL),
                  const((W_BR, D_MODEL)), const((W_BR, D_MODEL)), const((W_BR, D_MODEL)),
                  const((W_ATTN, D_MODEL)), const((D_MODEL, D_MODEL))],
        out_specs=[ctx_tile(D_MODEL), lat_tile(D_MODEL)],
        out_shape=[jax.ShapeDtypeStruct((N_CTX_TOK, D_MODEL), f32), jax.ShapeDtypeStruct((N_LAT_TOK, D_MODEL), f32)],
        compiler_params=_params(1, VMEM_LIMIT),
        name="merge_out",
    )(x_ctx, x_lat, mod_tiles_l, pa, pb_ctx, pb_lat, pc, att_ctx, att_lat, sa, sb, sc, sd, gm,
      woa, wob, woc, wod, wout)


def kernel(x_prompt, x_sample, cache_k, cache_v, c, c_ctx, w_ada, b_ada, norm_g, w_in, conv_dw_w, conv_dw_b, conv_ln_g, conv_ln_b, conv_pw, hy_short_w, hy_short_b, hy_w1, hy_b1, hy_freq, hy_w2, hy_b2, hy_w3, hy_b3, hy_skip, pool_w, pool_scale, q_norm, k_norm, wo_conv, wo_hyena, wo_pool, wo_attn, w_out):
    x_ctx, x_lat = x_prompt.reshape(N_CTX_TOK, D_MODEL), x_sample.reshape(N_LAT_TOK, D_MODEL)

    cond8 = jnp.concatenate([c_ctx[None, :], c, jnp.zeros((8 - 1 - DEC_BATCH, D_MODEL), f32)], axis=0)
    mod = _ada_call(cond8, w_ada, b_ada)
    tile_cond = np.concatenate([np.zeros(CTX_TILES, np.int32),
                                1 + np.arange(N_TILES - CTX_TILES, dtype=np.int32) // LAT_TPS])
    mod_tiles = mod[:, tile_cond].reshape(DEPTH, N_TILES, 3, D_MODEL)

    seq_lens = (SEQ, DEC_SEQ)
    tabs = {L: _dft_tables(2 * L // DFT_B) for L in seq_lens}
    feats = {L: _filter_features(L) for L in seq_lens}
    absd = _abs_deltas()
    cos_t, sin_t = _rope_tables()
    bd = _group_sum_matrix()
    w_in_bf = w_in.astype(bf16)
    w1p = jnp.pad(hy_w1, ((0, 0), (0, _FEAT_PAD - FILTER_EMB), (0, 0)))

    ks, vs = [], []
    for l in range(DEPTH):
        (pa, sa, v, x1, x2, sb, pc, sc, q, kv, sd, gm) = _inproj_call(
            x_ctx, x_lat, l, mod_tiles[l], norm_g[l], w_in_bf, conv_dw_w[l], conv_dw_b[l], conv_ln_g[l],
            conv_ln_b[l], conv_pw[l].astype(bf16), hy_short_w[l], hy_short_b[l], pool_w[l].astype(bf16),
            pool_scale[l])

        spec = {}
        for L in seq_lens:
            hfilt, ss = _filter_mlp_call(L, feats[L][0], feats[L][1], w1p[l], hy_b1[l], hy_freq[l], hy_w2[l],
                                         hy_b2[l], hy_w3[l], hy_b3[l], absd)
            spec[L] = _filter_spectrum(2 * L // DFT_B, tabs[L][0], hfilt, ss)
        pb_ctx = _hy_ctx_call(tabs[SEQ], spec[SEQ], v, x1, x2, hy_skip[l])
        pb_lat = _hy_lat_call(tabs[DEC_SEQ], spec[DEC_SEQ], v, x1, x2, hy_skip[l]).reshape(N_LAT_TOK, W_BR)

        qt, kn, khm, vthm = _qk_prep_call(q, kv, q_norm[l], k_norm[l], cos_t, sin_t, bd)
        att_ctx = _attn_call(qt, khm, vthm, BATCH, SEQ, N_Q_HEADS, SEQ, 0)
        cache = (cache_k[:, l].astype(bf16).transpose(0, 2, 1, 3), cache_v[:, l].astype(bf16).transpose(0, 2, 3, 1))
        att_lat = _attn_call(qt, khm, vthm, DEC_BATCH, DEC_SEQ, Q_PER_KV, 256, N_CTX_TOK, cache)

        x_ctx, x_lat = _merge_call(x_ctx, x_lat, mod_tiles[l], pa, pb_ctx, pb_lat, pc, att_ctx, att_lat, sa, sb, sc, sd, gm,
                        wo_conv[l].astype(bf16), wo_hyena[l].astype(bf16), wo_pool[l].astype(bf16),
                        wo_attn[l].astype(bf16), w_out[l].astype(bf16))

        ks.append(kn[:N_CTX_TOK].reshape(BATCH, SEQ, N_KV_HEADS, HEAD_DIM))
        vs.append(kv[:N_CTX_TOK, W_KV:].reshape(BATCH, SEQ, N_KV_HEADS, HEAD_DIM))

    y_prompt = x_ctx.reshape(BATCH, SEQ, D_MODEL)
    y_sample = x_lat.reshape(DEC_BATCH, DEC_SEQ, D_MODEL)
    return (y_prompt, y_sample, jnp.stack(ks, axis=1), jnp.stack(vs, axis=1))
```
